```python
import math
import jax, jax.numpy as jnp
from jax import lax
import numpy as np

D_MODEL = 1024
BATCH = 32
SEQ = 2048
DEPTH = 2

CONV_CH = 512
CONV_K = 31
DA_HEADS = 4
DA_SUB = 64
DA_QK = 2 * DA_SUB
DA_V = 128
ROT_DIM = DA_SUB // 4
ROPE_THETA = 500000.0
Q_BLOCK = 128
HG_HEADS = 4
HG_DK = 128
HG_DV = 128
HG_CHUNK = 32
MIX_WIDTH = 512
N_BRANCH = 3
N_GROUPS = 4
EXPERTS_PER_GROUP = 8
N_EXPERTS = N_GROUPS * EXPERTS_PER_GROUP
TOP_K = 2
D_EXPERT = 512
DN_ALPHA = (2 * DEPTH) ** 0.25
DN_BETA = (8 * DEPTH) ** -0.25
LN_EPS = 1e-5

SPLIT_SIZES = (
    2 * CONV_CH,
    DA_HEADS * DA_QK,
    DA_HEADS * DA_QK,
    DA_HEADS * DA_V,
    HG_HEADS * HG_DK,
    HG_HEADS * HG_DK,
    HG_HEADS * HG_DV,
    HG_HEADS * HG_DV,
    N_BRANCH * D_MODEL,
)
IN_COLS = sum(SPLIT_SIZES)
SPLIT_IDX = tuple(int(v) for v in np.cumsum(SPLIT_SIZES)[:-1])

kernel_name = "hybrid_conv_diffattn_hgrn2_hiermoe_deepnorm"


def layer_norm(x, g, b):
    xf = x.astype(jnp.float32)
    mu = jnp.mean(xf, -1, keepdims=True)
    var = jnp.mean(jnp.square(xf - mu), -1, keepdims=True)
    return ((xf - mu) * lax.rsqrt(var + LN_EPS)).astype(x.dtype) * g + b


def rms_norm(x, g):
    xf = x.astype(jnp.float32)
    return (xf * lax.rsqrt(jnp.mean(xf * xf, -1, keepdims=True) + LN_EPS)).astype(x.dtype) * g


def rotary_tables(positions, dtype):
    inv_freq = ROPE_THETA ** (-jnp.arange(0, ROT_DIM, 2, dtype=jnp.float32) / ROT_DIM)
    ang = positions.astype(jnp.float32)[..., None] * inv_freq
    return jnp.cos(ang).astype(dtype), jnp.sin(ang).astype(dtype)


def partial_rotary(t, cos, sin):
    half = ROT_DIM // 2
    t1, t2 = t[..., :half], t[..., half:ROT_DIM]
    c, s = cos[:, :, None, :], sin[:, :, None, :]
    return jnp.concatenate([t1 * c - t2 * s, t2 * c + t1 * s, t[..., ROT_DIM:]], -1)


def conformer_conv(a, conv_w, conv_b, ln_g, ln_b):
    u = a[..., :CONV_CH] * jax.nn.sigmoid(a[..., CONV_CH:])
    y = lax.conv_general_dilated(
        u, conv_w[:, None, :].astype(u.dtype), window_strides=(1,),
        padding=[(CONV_K - 1, 0)], dimension_numbers=("NWC", "WIO", "NWC"),
        feature_group_count=CONV_CH) + conv_b
    return jax.nn.silu(layer_norm(y, ln_g, ln_b))


def diff_attention(q, k, v, cos, sin, lam_params, norm_g, layer_idx):
    B, S, _ = q.shape
    q = partial_rotary(q.reshape(B, S, 2 * DA_HEADS, DA_SUB), cos, sin) * (DA_SUB ** -0.5)
    k = partial_rotary(k.reshape(B, S, 2 * DA_HEADS, DA_SUB), cos, sin)
    q = q.reshape(B, S, DA_HEADS, 2, DA_SUB)
    k = k.reshape(B, S, DA_HEADS, 2, DA_SUB)
    v = v.reshape(B, S, DA_HEADS, DA_V)
    lam_init = 0.8 - 0.6 * math.exp(-0.3 * layer_idx)
    lp = lam_params.astype(jnp.float32)
    lam = jnp.exp(jnp.sum(lp[0] * lp[1])) - jnp.exp(jnp.sum(lp[2] * lp[3])) + lam_init
    nb = S // Q_BLOCK
    qb = q.reshape(B, nb, Q_BLOCK, DA_HEADS, 2, DA_SUB).transpose(1, 0, 2, 3, 4, 5)
    key_pos = jnp.arange(S)

    def one_block(args):
        qi, i = args
        s = jnp.einsum("bqhpd,bkhpd->bhpqk", qi, k, preferred_element_type=jnp.float32)
        qpos = i * Q_BLOCK + jnp.arange(Q_BLOCK)
        s = jnp.where(key_pos[None, :] <= qpos[:, None], s, -jnp.inf)
        p = jax.nn.softmax(s, axis=-1)
        a = p[:, :, 0] - lam * p[:, :, 1]
        return jnp.einsum("bhqk,bkhd->bqhd", a.astype(v.dtype), v)

    o = lax.map(one_block, (qb, jnp.arange(nb)))
    o = o.transpose(1, 0, 2, 3, 4).reshape(B, S, DA_HEADS, DA_V)
    o = rms_norm(o, norm_g) * (1.0 - lam_init)
    return o.reshape(B, S, DA_HEADS * DA_V)


def hgrn2(q, f_logit, i_in, g, lb, norm_g):
    B, S, _ = q.shape
    C = HG_CHUNK
    nc = S // C
    qf = jax.nn.silu(q).astype(jnp.float32).reshape(B, S, HG_HEADS, HG_DK)
    lbf = lb.astype(jnp.float32)
    log_f = jnp.logaddexp(jnp.log(lbf), jnp.log1p(-lbf) + jax.nn.log_sigmoid(f_logit.astype(jnp.float32)))
    kf = -jnp.expm1(log_f)
    log_f = log_f.reshape(B, S, HG_HEADS, HG_DK)
    kf = kf.reshape(B, S, HG_HEADS, HG_DK)
    vf = i_in.astype(jnp.float32).reshape(B, S, HG_HEADS, HG_DV)

    def to_chunks(t):
        return t.reshape(B, nc, C, HG_HEADS, t.shape[-1]).transpose(1, 0, 3, 2, 4)

    bcum = jnp.cumsum(to_chunks(log_f), axis=-2)
    causal = jnp.tril(jnp.ones((C, C), dtype=bool))

    def step(state, xs):
        q_, k_, v_, b_ = xs
        o_inter = jnp.einsum("bhtc,bhcv->bhtv", q_ * jnp.exp(b_), state)
        diff = b_[:, :, :, None, :] - b_[:, :, None, :, :]
        decay = jnp.exp(jnp.where(causal[:, :, None], diff, -jnp.inf))
        att = jnp.einsum("bhtc,bhsc,bhtsc->bhts", q_, k_, decay)
        o_intra = jnp.einsum("bhts,bhsv->bhtv", att, v_)
        b_last = b_[:, :, -1:, :]
        k_dec = k_ * jnp.exp(b_last - b_)
        state = jnp.exp(b_last[:, :, 0, :])[..., None] * state + jnp.einsum("bhsc,bhsv->bhcv", k_dec, v_)
        return state, o_inter + o_intra

    s0 = jnp.zeros((B, HG_HEADS, HG_DK, HG_DV), jnp.float32)
    _, o = lax.scan(step, s0, (to_chunks(qf), to_chunks(kf), to_chunks(vf), bcum))
    o = o.transpose(1, 0, 3, 2, 4).reshape(B, S, HG_HEADS, HG_DV).astype(q.dtype)
    o = rms_norm(o, norm_g) * jax.nn.silu(g.reshape(B, S, HG_HEADS, HG_DV))
    return o.reshape(B, S, HG_HEADS * HG_DV)


def hier_moe(x, wg, bg, we, be, w1, w3, w2):
    B, S, D = x.shape
    t = x.reshape(-1, D)
    glog = (t @ wg).astype(jnp.float32) + bg
    gprob = jax.nn.softmax(glog, axis=-1)
    g_idx = jnp.argmax(glog, axis=-1)
    p_group = jnp.take_along_axis(gprob, g_idx[:, None], axis=-1)
    elog_all = jnp.einsum("td,gde->tge", t, we).astype(jnp.float32) + be
    elog = jnp.take_along_axis(elog_all, g_idx[:, None, None], axis=1)[:, 0]
    top_v, top_i = lax.top_k(elog, TOP_K)
    w_top = jax.nn.softmax(top_v, axis=-1) * p_group
    expert_id = g_idx[:, None] * EXPERTS_PER_GROUP + top_i
    comb = jnp.sum(jax.nn.one_hot(expert_id, N_EXPERTS, dtype=jnp.float32) * w_top[..., None], axis=1)
    comb = comb.astype(t.dtype)
    out = jnp.zeros_like(t)
    for e in range(N_EXPERTS):
        h = jax.nn.silu(t @ w1[e]) * (t @ w3[e])
        out = out + comb[:, e:e + 1] * (h @ w2[e])
    return out.reshape(B, S, D)


def setup_inputs(seed: int = 0) -> dict:
    key = jax.random.key(seed)
    ks = jax.random.split(key, 26)
    f32 = jnp.float32
    nrm = lambda k, shape, scale: jax.random.normal(k, shape, f32) * scale
    L, D = DEPTH, D_MODEL
    start = jax.random.randint(ks[1], (BATCH,), 0, 4096, dtype=jnp.int32)
    positions = (start[:, None] + jnp.arange(SEQ, dtype=jnp.int32)[None, :]).astype(jnp.int32)
    return {
        "x": jax.random.normal(ks[0], (BATCH, SEQ, D), f32),
        "positions": positions,
        "w_in": nrm(ks[2], (L, D, IN_COLS), D ** -0.5),
        "conv_w": nrm(ks[3], (L, CONV_K, CONV_CH), CONV_K ** -0.5),
        "conv_b": nrm(ks[4], (L, CONV_CH), 0.02),
        "conv_ln_g": 1.0 + nrm(ks[5], (L, CONV_CH), 0.02),
        "conv_ln_b": nrm(ks[6], (L, CONV_CH), 0.02),
        "da_lambda": nrm(ks[7], (L, 4, DA_SUB), 0.1),
        "da_norm_g": 1.0 + nrm(ks[8], (L, DA_V), 0.02),
        "hg_lb": nrm(ks[9], (L, HG_HEADS * HG_DK), 0.5),
        "hg_norm_g": 1.0 + nrm(ks[10], (L, HG_DV), 0.02),
        "w_branch": nrm(ks[11], (L, N_BRANCH, MIX_WIDTH, D), MIX_WIDTH ** -0.5),
        "b_gate": nrm(ks[12], (L, N_BRANCH, D), 0.02),
        "w_out": nrm(ks[13], (L, D, D), D ** -0.5 * DN_BETA),
        "ln1_g": 1.0 + nrm(ks[14], (L, D), 0.02),
        "ln1_b": nrm(ks[15], (L, D), 0.02),
        "router_group": nrm(ks[16], (L, D, N_GROUPS), D ** -0.5),
        "router_group_b": nrm(ks[17], (L, N_GROUPS), 0.01),
        "router_expert": nrm(ks[18], (L, N_GROUPS, D, EXPERTS_PER_GROUP), D ** -0.5),
        "router_expert_b": nrm(ks[19], (L, N_GROUPS, EXPERTS_PER_GROUP), 0.01),
        "exp_w1": nrm(ks[20], (L, N_EXPERTS, D, D_EXPERT), D ** -0.5),
        "exp_w3": nrm(ks[21], (L, N_EXPERTS, D, D_EXPERT), D ** -0.5),
        "exp_w2": nrm(ks[22], (L, N_EXPERTS, D_EXPERT, D), D_EXPERT ** -0.5 * DN_BETA),
        "ln2_g": 1.0 + nrm(ks[23], (L, D), 0.02),
        "ln2_b": nrm(ks[24], (L, D), 0.02),
    }


def reference(x, positions, w_in, conv_w, conv_b, conv_ln_g, conv_ln_b, da_lambda, da_norm_g,
              hg_lb, hg_norm_g, w_branch, b_gate, w_out, ln1_g, ln1_b, router_group,
              router_group_b, router_expert, router_expert_b, exp_w1, exp_w3, exp_w2,
              ln2_g, ln2_b):
    B, S, D = x.shape
    cos, sin = rotary_tables(positions, x.dtype)
    lb_tab = jnp.cumsum(jax.nn.softmax(hg_lb.astype(jnp.float32), axis=0), axis=0)
    lb_tab = lb_tab - lb_tab[0:1]
    for l in range(DEPTH):
        h = x @ w_in[l]
        a_in, q_a, k_a, v_a, q_h, f_h, i_h, g_h, gate_logits = jnp.split(h, SPLIT_IDX, axis=-1)
        y_a = conformer_conv(a_in, conv_w[l], conv_b[l], conv_ln_g[l], conv_ln_b[l])
        y_b = diff_attention(q_a, k_a, v_a, cos, sin, da_lambda[l], da_norm_g[l], l)
        y_c = hgrn2(q_h, f_h, i_h, g_h, lb_tab[l], hg_norm_g[l])
        ys = jnp.stack([y_a, y_b, y_c], axis=2)
        proj = jnp.einsum("bsnc,ncd->bsnd", ys, w_branch[l])
        gates = jax.nn.sigmoid(gate_logits.reshape(B, S, N_BRANCH, D) + b_gate[l])
        mix = jnp.sum(gates * proj, axis=2) @ w_out[l]
        x = layer_norm(DN_ALPHA * x + mix, ln1_g[l], ln1_b[l])
        moe = hier_moe(x, router_group[l], router_group_b[l], router_expert[l], router_expert_b[l],
                       exp_w1[l], exp_w3[l], exp_w2[l])
        x = layer_norm(DN_ALPHA * x + moe, ln2_g[l], ln2_b[l])
    return x
```

```python
import functools
import math

import jax
import jax.numpy as jnp
from jax import lax
from jax.experimental import pallas as pl
from jax.experimental.pallas import tpu as pltpu

F32 = jnp.float32
BF16 = jnp.bfloat16

D_MODEL = 1024
DEPTH = 2
CONV_CH = 512
CONV_K = 31
DA_HEADS = 4
DA_SUB = 64
ROT_DIM = DA_SUB // 4
ROPE_THETA = 500000.0
HG_HEADS = 4
HG_DK = 128
MIX_WIDTH = 512
N_BRANCH = 3
N_GROUPS = 4
EXPERTS_PER_GROUP = 8
N_EXPERTS = N_GROUPS * EXPERTS_PER_GROUP
D_EXPERT = 512
DN_ALPHA = (2 * DEPTH) ** 0.25
LN_EPS = 1e-5
IN_COLS = 2 * CONV_CH + 7 * MIX_WIDTH + N_BRANCH * D_MODEL

LANES = 128
SUBLANES = 8
VMEM_LIMIT_BYTES = 56 * 1024 * 1024

NEG_BIG = -1e30


def _cparams(sem):
    return pltpu.CompilerParams(dimension_semantics=sem, vmem_limit_bytes=VMEM_LIMIT_BYTES)


def _layer_norm_rows(r, g, b):
    mu = jnp.mean(r, axis=-1, keepdims=True)
    d = r - mu
    var = jnp.mean(d * d, axis=-1, keepdims=True)
    return d * lax.rsqrt(var + LN_EPS) * g + b


def _sigmoid(x):
    return 1.0 / (1.0 + jnp.exp(-x))


def _rope_kernel(pos_ref, freq_ref, c_ref, sa_ref, sb_ref):
    ang = pos_ref[...].astype(F32) * freq_ref[...]
    lane = lax.broadcasted_iota(jnp.int32, ang.shape, 1) & (DA_SUB - 1)
    half = ROT_DIM // 2
    cs = jnp.cos(ang)
    sn = jnp.sin(ang)
    c_ref[...] = jnp.where(lane < ROT_DIM, cs, 1.0)
    sa_ref[...] = jnp.where(lane < half, -sn, 0.0)
    sb_ref[...] = jnp.where((lane >= half) & (lane < ROT_DIM), sn, 0.0)


def _rope_tables(positions):
    T = positions.size
    tm = min(T, 1024)
    inv_freq = ROPE_THETA ** (-jnp.arange(0, ROT_DIM, 2, dtype=F32) / ROT_DIM)
    freq_lane = jnp.tile(inv_freq, LANES // (ROT_DIM // 2))[None, :]
    pos = positions.reshape(T, 1)
    out = jax.ShapeDtypeStruct((T, LANES), F32)
    return pl.pallas_call(
        _rope_kernel,
        grid=(T // tm,),
        in_specs=[pl.BlockSpec((tm, 1), lambda i: (i, 0)),
                  pl.BlockSpec((1, LANES), lambda i: (0, 0))],
        out_specs=[pl.BlockSpec((tm, LANES), lambda i: (i, 0))] * 3,
        out_shape=[out, out, out],
        compiler_params=_cparams(("parallel",)),
        name="rope_tables",
    )(pos, freq_lane)


_COL_A = 0
_COL_Q = 2 * CONV_CH
_COL_K = _COL_Q + MIX_WIDTH
_COL_V = _COL_K + MIX_WIDTH
_COL_HQ = _COL_V + MIX_WIDTH
_COL_HF = _COL_HQ + MIX_WIDTH
_COL_HI = _COL_HF + MIX_WIDTH
_COL_HG = _COL_HI + MIX_WIDTH
_COL_GATE = _COL_HG + MIX_WIDTH


def _inproj_kernel(x_ref, w_ref, c_ref, sa_ref, sb_ref,
                   a_ref, q_ref, k_ref, v_ref, hq_ref, hf_ref, hi_ref, hg_ref, gt_ref):
    xb = x_ref[...].astype(BF16)

    def mm(c0, width):
        return jnp.dot(xb, w_ref[:, c0:c0 + width], preferred_element_type=F32)

    def rotary(t, scale):
        c, sa, sb = c_ref[...], sa_ref[...], sb_ref[...]
        outs = []
        for h in range(DA_HEADS):
            th = t[:, h * LANES:(h + 1) * LANES]
            r = th * c + pltpu.roll(th, LANES - ROT_DIM // 2, 1) * sa + pltpu.roll(th, ROT_DIM // 2, 1) * sb
            outs.append(r * scale)
        return jnp.concatenate(outs, axis=1)

    for j in range(2):
        a_ref[:, j * CONV_CH:(j + 1) * CONV_CH] = mm(_COL_A + j * CONV_CH, CONV_CH).astype(BF16)
    q_ref[...] = rotary(mm(_COL_Q, MIX_WIDTH), DA_SUB ** -0.5).astype(BF16)
    k_ref[...] = rotary(mm(_COL_K, MIX_WIDTH), 1.0).astype(BF16)
    v_ref[...] = mm(_COL_V, MIX_WIDTH).astype(BF16)
    hq_ref[...] = mm(_COL_HQ, MIX_WIDTH).astype(BF16)
    hf_ref[...] = mm(_COL_HF, MIX_WIDTH).astype(BF16)
    hi_ref[...] = mm(_COL_HI, MIX_WIDTH).astype(BF16)
    hg_ref[...] = mm(_COL_HG, MIX_WIDTH).astype(BF16)
    for j in range(N_BRANCH * D_MODEL // MIX_WIDTH):
        gt_ref[:, j * MIX_WIDTH:(j + 1) * MIX_WIDTH] = mm(_COL_GATE + j * MIX_WIDTH, MIX_WIDTH).astype(BF16)


def _in_projection(x2, w_in_bf16, rope):
    T = x2.shape[0]
    tm = min(T, 512)
    widths = [2 * CONV_CH] + [MIX_WIDTH] * 7 + [N_BRANCH * D_MODEL]
    row = lambda i: (i, 0)
    return pl.pallas_call(
        _inproj_kernel,
        grid=(T // tm,),
        in_specs=[pl.BlockSpec((tm, D_MODEL), row),
                  pl.BlockSpec((D_MODEL, IN_COLS), lambda i: (0, 0), pipeline_mode=pl.Buffered(1)),
                  pl.BlockSpec((tm, LANES), row),
                  pl.BlockSpec((tm, LANES), row),
                  pl.BlockSpec((tm, LANES), row)],
        out_specs=[pl.BlockSpec((tm, w), row) for w in widths],
        out_shape=[jax.ShapeDtypeStruct((T, w), BF16) for w in widths],
        compiler_params=_cparams(("parallel",)),
        name="in_projection",
    )(x2, w_in_bf16, *rope)


_HALO = 32
_CONV_ROWS = 32


def _conv_kernel(a_ref, halo_ref, w_ref, cb_ref, g_ref, b_ref, o_ref, ush_ref, y_ref, *, ts):
    j = pl.program_id(1)

    def glu(a):
        a = a.astype(F32)
        return a[:, :CONV_CH] * _sigmoid(a[:, CONV_CH:])

    u_cur = glu(a_ref[...])
    u_halo = jnp.where(j > 0, glu(halo_ref[...]), 0.0)
    u_ext = jnp.concatenate([u_halo, u_cur], axis=0)
    n_al = ts + _HALO - SUBLANES
    ush_ref[0] = u_ext
    for s in range(1, SUBLANES):
        ush_ref[s, 0:n_al, :] = u_ext[s:s + n_al, :]

    off0 = _HALO - (CONV_K - 1)

    def rows(r, carry):
        r0 = pl.multiple_of(r * _CONV_ROWS, _CONV_ROWS)
        for cb in range(CONV_CH // LANES):
            cs = slice(cb * LANES, (cb + 1) * LANES)
            acc = jnp.zeros((_CONV_ROWS, LANES), F32)
            for k in range(CONV_K):
                off = off0 + k
                s, al = off % SUBLANES, (off // SUBLANES) * SUBLANES
                acc = acc + w_ref[k:k + 1, cs] * ush_ref[s, pl.ds(r0 + al, _CONV_ROWS), cs]
            y_ref[pl.ds(r0, _CONV_ROWS), cs] = acc
        return carry

    lax.fori_loop(0, ts // _CONV_ROWS, rows, 0)
    y = y_ref[...] + cb_ref[...]
    yn = _layer_norm_rows(y, g_ref[...], b_ref[...])
    o_ref[...] = (yn * _sigmoid(yn)).astype(BF16)


def _conformer_conv(a_in, conv_w, conv_b, ln_g, ln_b, B, S):
    T = B * S
    ts = min(S, 512)
    nt = S // ts
    vec = lambda v: v.reshape(1, CONV_CH)
    return pl.pallas_call(
        functools.partial(_conv_kernel, ts=ts),
        grid=(B, nt),
        in_specs=[pl.BlockSpec((ts, 2 * CONV_CH), lambda b, j: (b * nt + j, 0)),
                  pl.BlockSpec((_HALO, 2 * CONV_CH),
                               lambda b, j: (jnp.maximum((b * nt + j) * (ts // _HALO) - 1, 0), 0)),
                  pl.BlockSpec((CONV_K, CONV_CH), lambda b, j: (0, 0)),
                  pl.BlockSpec((1, CONV_CH), lambda b, j: (0, 0)),
                  pl.BlockSpec((1, CONV_CH), lambda b, j: (0, 0)),
                  pl.BlockSpec((1, CONV_CH), lambda b, j: (0, 0))],
        out_specs=pl.BlockSpec((ts, CONV_CH), lambda b, j: (b * nt + j, 0)),
        out_shape=jax.ShapeDtypeStruct((T, CONV_CH), BF16),
        scratch_shapes=[pltpu.VMEM((SUBLANES, ts + _HALO, CONV_CH), F32),
                        pltpu.VMEM((ts, CONV_CH), F32)],
        compiler_params=_cparams(("parallel", "parallel")),
        name="conformer_conv",
    )(a_in, a_in, conv_w, vec(conv_b), vec(ln_g), vec(ln_b))


def _attn_kernel(q_ref, k_ref, v_ref, lam_ref, g_ref, o_ref, *, tq, tk, lam_init):
    qi = pl.program_id(2)
    q = q_ref[...]
    lane = lax.broadcasted_iota(jnp.int32, q.shape, 1)
    zero = jnp.zeros_like(q)
    qq = jnp.concatenate([jnp.where(lane < DA_SUB, q, zero), jnp.where(lane >= DA_SUB, q, zero)], axis=0)
    r_in = lax.broadcasted_iota(jnp.int32, (tq, 1), 0)
    row_pos = qi * tq + jnp.concatenate([r_in, r_in], axis=0)

    def step(j, carry):
        m, l, acc = carry
        k0 = pl.multiple_of(j * tk, tk)
        kb = k_ref[pl.ds(k0, tk), :]
        vb = v_ref[pl.ds(k0, tk), :]
        s = lax.dot_general(qq, kb, (((1,), (1,)), ((), ())), preferred_element_type=F32)
        col = k0 + lax.broadcasted_iota(jnp.int32, (1, tk), 1)
        s = jnp.where(col <= row_pos, s, NEG_BIG)
        m_new = jnp.maximum(m, jnp.max(s, axis=1, keepdims=True))
        alpha = jnp.exp(m - m_new)
        p = jnp.exp(s - m_new)
        l = alpha * l + jnp.sum(p, axis=1, keepdims=True)
        acc = alpha * acc + jnp.dot(p.astype(BF16), vb, preferred_element_type=F32)
        return m_new, l, acc

    n_kv = (qi * tq + tq + tk - 1) // tk
    init = (jnp.full((2 * tq, 1), NEG_BIG, F32), jnp.zeros((2 * tq, 1), F32),
            jnp.zeros((2 * tq, LANES), F32))
    _, l, acc = lax.fori_loop(0, n_kv, step, init)
    o12 = acc / l
    lp = lam_ref[...]
    lam = (jnp.exp(jnp.sum(lp[0:1] * lp[1:2], axis=1, keepdims=True))
           - jnp.exp(jnp.sum(lp[2:3] * lp[3:4], axis=1, keepdims=True)) + lam_init)
    o = o12[:tq] - lam * o12[tq:]
    ms = jnp.mean(o * o, axis=1, keepdims=True)
    o_ref[...] = (o * lax.rsqrt(ms + LN_EPS) * g_ref[...] * (1.0 - lam_init)).astype(BF16)


def _diff_attention(q, k, v, lam_params, norm_g, layer_idx, B, S):
    T = B * S
    tq = min(S, 256)
    tk = min(S, 256)
    nq = S // tq
    lam_init = 0.8 - 0.6 * math.exp(-0.3 * layer_idx)
    return pl.pallas_call(
        functools.partial(_attn_kernel, tq=tq, tk=tk, lam_init=lam_init),
        grid=(B, DA_HEADS, nq),
        in_specs=[pl.BlockSpec((tq, LANES), lambda b, h, i: (b * nq + i, h)),
                  pl.BlockSpec((S, LANES), lambda b, h, i: (b, h)),
                  pl.BlockSpec((S, LANES), lambda b, h, i: (b, h)),
                  pl.BlockSpec((4, DA_SUB), lambda b, h, i: (0, 0)),
                  pl.BlockSpec((1, LANES), lambda b, h, i: (0, 0))],
        out_specs=pl.BlockSpec((tq, LANES), lambda b, h, i: (b * nq + i, h)),
        out_shape=jax.ShapeDtypeStruct((T, MIX_WIDTH), BF16),
        compiler_params=_cparams(("parallel", "parallel", "arbitrary")),
        name="diff_attention",
    )(q, k, v, lam_params, norm_g.reshape(1, LANES))


_HG_CHUNK = 64


def _hgrn_kernel(q_ref, f_ref, i_ref, g_ref, lb_ref, ng_ref, o_ref, st_ref, *, n_chunks):
    C = _HG_CHUNK

    @pl.when(pl.program_id(1) == 0)
    def _():
        st_ref[...] = jnp.zeros_like(st_ref)

    rr = lax.broadcasted_iota(jnp.int32, (C, C), 0)
    cc = lax.broadcasted_iota(jnp.int32, (C, C), 1)
    causal = rr >= cc
    tri = jnp.where(causal, 1.0, 0.0).astype(BF16)

    for ci in range(n_chunks):
        rows = slice(ci * C, (ci + 1) * C)
        for h in range(HG_HEADS):
            cols = slice(h * HG_DK, (h + 1) * HG_DK)
            z = f_ref[rows, cols].astype(F32)
            lb = lb_ref[:, cols]
            e = jnp.exp(-jnp.abs(z))
            log_sig = jnp.minimum(z, 0.0) - jnp.log(1.0 + e)
            ya = jnp.log(lb)
            yb = jnp.log(1.0 - lb) + log_sig
            log_f = jnp.maximum(ya, yb) + jnp.log(1.0 + jnp.exp(-jnp.abs(ya - yb)))
            kf = (1.0 - lb) * jnp.where(z >= 0.0, e, 1.0) / (1.0 + e)
            hi = log_f.astype(BF16)
            r1 = log_f - hi.astype(F32)
            mid = r1.astype(BF16)
            lo = (r1 - mid.astype(F32)).astype(BF16)
            b3 = jnp.dot(tri, jnp.concatenate([hi, mid, lo], axis=1), preferred_element_type=F32)
            bcum = b3[:, 0:HG_DK] + b3[:, HG_DK:2 * HG_DK] + b3[:, 2 * HG_DK:3 * HG_DK]
            b_last = bcum[C - 1:C, :]
            mref = 0.5 * b_last
            qv = q_ref[rows, cols].astype(F32)
            qf = qv * _sigmoid(qv)
            qe = qf * jnp.exp(bcum)
            qd = (qe * jnp.exp(-mref)).astype(BF16)
            kd = kf * jnp.exp(mref - bcum)
            kdec = (kd * jnp.exp(mref)).astype(BF16)
            vv = i_ref[rows, cols]
            att = lax.dot_general(qd, kd.astype(BF16), (((1,), (1,)), ((), ())),
                                  preferred_element_type=F32)
            att = jnp.where(causal, att, 0.0).astype(BF16)
            st = st_ref[h]
            o = (lax.dot_general(qe.astype(BF16), st.astype(BF16), (((1,), (1,)), ((), ())),
                                 preferred_element_type=F32)
                 + jnp.dot(att, vv, preferred_element_type=F32))
            st_ref[h] = st * jnp.exp(b_last) + lax.dot_general(
                vv, kdec, (((0,), (0,)), ((), ())), preferred_element_type=F32)
            ms = jnp.mean(o * o, axis=1, keepdims=True)
            gv = g_ref[rows, cols].astype(F32)
            o_ref[rows, cols] = (o * lax.rsqrt(ms + LN_EPS) * ng_ref[...] * (gv * _sigmoid(gv))).astype(BF16)


def _hgrn2(hq, hf, hi, hg, lb, norm_g, B, S):
    T = B * S
    ts = min(S, 256)
    nt = S // ts
    blk = pl.BlockSpec((ts, MIX_WIDTH), lambda b, j: (b * nt + j, 0))
    return pl.pallas_call(
        functools.partial(_hgrn_kernel, n_chunks=ts // _HG_CHUNK),
        grid=(B, nt),
        in_specs=[blk, blk, blk, blk,
                  pl.BlockSpec((1, MIX_WIDTH), lambda b, j: (0, 0)),
                  pl.BlockSpec((1, HG_DK), lambda b, j: (0, 0))],
        out_specs=blk,
        out_shape=jax.ShapeDtypeStruct((T, MIX_WIDTH), BF16),
        scratch_shapes=[pltpu.VMEM((HG_HEADS, HG_DK, HG_DK), F32)],
        compiler_params=_cparams(("parallel", "arbitrary")),
        name="hgrn2",
    )(hq, hf, hi, hg, lb.reshape(1, MIX_WIDTH), norm_g.reshape(1, HG_DK))


_ROUTE_COLS = N_GROUPS + N_EXPERTS


def _merge_kernel(x_ref, ya_ref, yb_ref, yc_ref, gt_ref, wbr_ref, bg_ref, wout_ref, lng_ref, lnb_ref,
                  wr_ref, br_ref, x1_ref, comb_ref):
    mix = None
    for n, y_ref in enumerate((ya_ref, yb_ref, yc_ref)):
        proj = jnp.dot(y_ref[...], wbr_ref[n], preferred_element_type=F32)
        gate = _sigmoid(gt_ref[:, n * D_MODEL:(n + 1) * D_MODEL].astype(F32) + bg_ref[n:n + 1, :])
        mix = gate * proj if mix is None else mix + gate * proj
    mo = jnp.dot(mix.astype(BF16), wout_ref[...], preferred_element_type=F32)
    x1 = _layer_norm_rows(DN_ALPHA * x_ref[...] + mo, lng_ref[...], lnb_ref[...])
    x1_ref[...] = x1

    logits = jnp.dot(x1, wr_ref[...], preferred_element_type=F32,
                     precision=lax.Precision.HIGHEST) + br_ref[...]
    lane = lax.broadcasted_iota(jnp.int32, logits.shape, 1)
    big = jnp.int32(1 << 20)
    is_g = lane < N_GROUPS
    gl = jnp.where(is_g, logits, NEG_BIG)
    gmax = jnp.max(gl, axis=1, keepdims=True)
    g_idx = jnp.min(jnp.where(is_g & (gl == gmax), lane, big), axis=1, keepdims=True)
    p_group = 1.0 / jnp.sum(jnp.where(is_g, jnp.exp(gl - gmax), 0.0), axis=1, keepdims=True)
    lo = N_GROUPS + EXPERTS_PER_GROUP * g_idx
    sel = (lane >= lo) & (lane < lo + EXPERTS_PER_GROUP)
    el = jnp.where(sel, logits, NEG_BIG)
    v1 = jnp.max(el, axis=1, keepdims=True)
    i1 = jnp.min(jnp.where(sel & (el == v1), lane, big), axis=1, keepdims=True)
    sel2 = sel & (lane != i1)
    el2 = jnp.where(sel2, logits, NEG_BIG)
    v2 = jnp.max(el2, axis=1, keepdims=True)
    i2 = jnp.min(jnp.where(sel2 & (el2 == v2), lane, big), axis=1, keepdims=True)
    e2 = jnp.exp(v2 - v1)
    w1 = p_group / (1.0 + e2)
    w2 = p_group * e2 / (1.0 + e2)
    comb_ref[...] = (jnp.where(lane == i1 - N_GROUPS, w1, 0.0)
                     + jnp.where(lane == i2 - N_GROUPS, w2, 0.0))


def _merge(x2, ya, yb, yc, gates, w_branch, b_gate, w_out, ln_g, ln_b, w_route, b_route):
    T = x2.shape[0]
    tm = min(T, 512)
    row = lambda i: (i, 0)
    const2 = lambda i: (0, 0)
    return pl.pallas_call(
        _merge_kernel,
        grid=(T // tm,),
        in_specs=[pl.BlockSpec((tm, D_MODEL), row),
                  pl.BlockSpec((tm, MIX_WIDTH), row),
                  pl.BlockSpec((tm, MIX_WIDTH), row),
                  pl.BlockSpec((tm, MIX_WIDTH), row),
                  pl.BlockSpec((tm, N_BRANCH * D_MODEL), row),
                  pl.BlockSpec((N_BRANCH, MIX_WIDTH, D_MODEL), lambda i: (0, 0, 0)),
                  pl.BlockSpec((N_BRANCH, D_MODEL), const2),
                  pl.BlockSpec((D_MODEL, D_MODEL), const2),
                  pl.BlockSpec((1, D_MODEL), const2),
                  pl.BlockSpec((1, D_MODEL), const2),
                  pl.BlockSpec((D_MODEL, LANES), const2),
                  pl.BlockSpec((1, LANES), const2)],
        out_specs=[pl.BlockSpec((tm, D_MODEL), row), pl.BlockSpec((tm, LANES), row)],
        out_shape=[jax.ShapeDtypeStruct((T, D_MODEL), F32), jax.ShapeDtypeStruct((T, LANES), F32)],
        compiler_params=_cparams(("parallel",)),
        name="merge_router",
    )(x2, ya, yb, yc, gates, w_branch, b_gate, w_out, ln_g.reshape(1, -1), ln_b.reshape(1, -1),
      w_route, b_route)


def _moe_kernel(x1_ref, comb_ref, w1_ref, w3_ref, w2_ref, lng_ref, lnb_ref, o_ref, acc_ref):
    e = pl.program_id(1)

    @pl.when(e == 0)
    def _():
        acc_ref[...] = jnp.zeros_like(acc_ref)

    xb = x1_ref[...].astype(BF16)
    h1 = jnp.dot(xb, w1_ref[0], preferred_element_type=F32)
    h3 = jnp.dot(xb, w3_ref[0], preferred_element_type=F32)
    hh = (h1 * _sigmoid(h1) * h3).astype(BF16)
    y = jnp.dot(hh, w2_ref[0], preferred_element_type=F32)
    comb = comb_ref[...]
    lane = lax.broadcasted_iota(jnp.int32, comb.shape, 1)
    we = jnp.sum(jnp.where(lane == e, comb, 0.0), axis=1, keepdims=True)
    acc_ref[...] += we * y

    @pl.when(e == N_EXPERTS - 1)
    def _():
        o_ref[...] = _layer_norm_rows(DN_ALPHA * x1_ref[...] + acc_ref[...], lng_ref[...], lnb_ref[...])


def _moe(x1, comb, w1, w3, w2, ln_g, ln_b):
    T = x1.shape[0]
    tm = min(T, 1024)
    row = lambda i, e: (i, 0)
    return pl.pallas_call(
        _moe_kernel,
        grid=(T // tm, N_EXPERTS),
        in_specs=[pl.BlockSpec((tm, D_MODEL), row),
                  pl.BlockSpec((tm, LANES), row),
                  pl.BlockSpec((1, D_MODEL, D_EXPERT), lambda i, e: (e, 0, 0)),
                  pl.BlockSpec((1, D_MODEL, D_EXPERT), lambda i, e: (e, 0, 0)),
                  pl.BlockSpec((1, D_EXPERT, D_MODEL), lambda i, e: (e, 0, 0)),
                  pl.BlockSpec((1, D_MODEL), lambda i, e: (0, 0)),
                  pl.BlockSpec((1, D_MODEL), lambda i, e: (0, 0))],
        out_specs=pl.BlockSpec((tm, D_MODEL), row),
        out_shape=jax.ShapeDtypeStruct((T, D_MODEL), F32),
        scratch_shapes=[pltpu.VMEM((tm, D_MODEL), F32)],
        compiler_params=_cparams(("parallel", "arbitrary")),
        name="moe_experts",
    )(x1, comb, w1, w3, w2, ln_g.reshape(1, -1), ln_b.reshape(1, -1))


def _router_weights(router_group, router_group_b, router_expert, router_expert_b):
    we = jnp.transpose(router_expert, (1, 0, 2)).reshape(D_MODEL, N_EXPERTS)
    w = jnp.concatenate([router_group, we], axis=1)
    b = jnp.concatenate([router_group_b, router_expert_b.reshape(N_EXPERTS)])
    pad = LANES - _ROUTE_COLS
    return jnp.pad(w, ((0, 0), (0, pad))), jnp.pad(b, (0, pad)).reshape(1, LANES)


def kernel(x, positions, w_in, conv_w, conv_b, conv_ln_g, conv_ln_b, da_lambda, da_norm_g, hg_lb, hg_norm_g, w_branch, b_gate, w_out, ln1_g, ln1_b, router_group, router_group_b, router_expert, router_expert_b, exp_w1, exp_w3, exp_w2, ln2_g, ln2_b):
    B, S, D = x.shape
    T = B * S
    rope = _rope_tables(positions)
    lb_tab = jnp.cumsum(jax.nn.softmax(hg_lb.astype(F32), axis=0), axis=0)
    lb_tab = lb_tab - lb_tab[0:1]
    x2 = x.reshape(T, D)
    for l in range(DEPTH):
        a_in, q, k, v, hq, hf, hi, hg, gates = _in_projection(x2, w_in[l].astype(BF16), rope)
        y_a = _conformer_conv(a_in, conv_w[l], conv_b[l], conv_ln_g[l], conv_ln_b[l], B, S)
        y_b = _diff_attention(q, k, v, da_lambda[l], da_norm_g[l], l, B, S)
        y_c = _hgrn2(hq, hf, hi, hg, lb_tab[l], hg_norm_g[l], B, S)
        w_route, b_route = _router_weights(router_group[l], router_group_b[l],
                                           router_expert[l], router_expert_b[l])
        x1, comb = _merge(x2, y_a, y_b, y_c, gates, w_branch[l].astype(BF16), b_gate[l],
                          w_out[l].astype(BF16), ln1_g[l], ln1_b[l], w_route, b_route)
        x2 = _moe(x1, comb, exp_w1[l].astype(BF16), exp_w3[l].astype(BF16), exp_w2[l].astype(BF16),
                  ln2_g[l], ln2_b[l])
    return x2.reshape(B, S, D)
```

```python
import functools
import math

import jax
import jax.numpy as jnp
from jax import lax
from jax.experimental import pallas as pl
from jax.experimental.pallas import tpu as pltpu

F32 = jnp.float32
BF16 = jnp.bfloat16

D_MODEL = 1024
DEPTH = 2
CONV_CH = 512
CONV_K = 31
DA_HEADS = 4
DA_SUB = 64
ROT_DIM = DA_SUB // 4
ROPE_THETA = 500000.0
HG_HEADS = 4
HG_DK = 128
MIX_WIDTH = 512
N_BRANCH = 3
N_GROUPS = 4
EXPERTS_PER_GROUP = 8
N_EXPERTS = N_GROUPS * EXPERTS_PER_GROUP
D_EXPERT = 512
DN_ALPHA = (2 * DEPTH) ** 0.25
LN_EPS = 1e-5
IN_COLS = 2 * CONV_CH + 7 * MIX_WIDTH + N_BRANCH * D_MODEL

LANES = 128
SUBLANES = 8
VMEM_LIMIT_BYTES = 56 * 1024 * 1024

NEG_BIG = -1e30


def _cparams(sem):
    return pltpu.CompilerParams(dimension_semantics=sem, vmem_limit_bytes=VMEM_LIMIT_BYTES)


def _layer_norm_rows(r, g, b):
    mu = jnp.mean(r, axis=-1, keepdims=True)
    d = r - mu
    var = jnp.mean(d * d, axis=-1, keepdims=True)
    return d * lax.rsqrt(var + LN_EPS) * g + b


def _sigmoid(x):
    return 1.0 / (1.0 + jnp.exp(-x))


def _rope_kernel(pos_ref, freq_ref, c_ref, sa_ref, sb_ref):
    ang = pos_ref[...].astype(F32) * freq_ref[...]
    lane = lax.broadcasted_iota(jnp.int32, ang.shape, 1) & (DA_SUB - 1)
    half = ROT_DIM // 2
    cs = jnp.cos(ang)
    sn = jnp.sin(ang)
    c_ref[...] = jnp.where(lane < ROT_DIM, cs, 1.0)
    sa_ref[...] = jnp.where(lane < half, -sn, 0.0)
    sb_ref[...] = jnp.where((lane >= half) & (lane < ROT_DIM), sn, 0.0)


def _rope_tables(positions):
    T = positions.size
    tm = min(T, 1024)
    inv_freq = ROPE_THETA ** (-jnp.arange(0, ROT_DIM, 2, dtype=F32) / ROT_DIM)
    freq_lane = jnp.tile(inv_freq, LANES // (ROT_DIM // 2))[None, :]
    pos = positions.reshape(T, 1)
    out = jax.ShapeDtypeStruct((T, LANES), F32)
    return pl.pallas_call(
        _rope_kernel,
        grid=(T // tm,),
        in_specs=[pl.BlockSpec((tm, 1), lambda i: (i, 0)),
                  pl.BlockSpec((1, LANES), lambda i: (0, 0))],
        out_specs=[pl.BlockSpec((tm, LANES), lambda i: (i, 0))] * 3,
        out_shape=[out, out, out],
        compiler_params=_cparams(("parallel",)),
        name="rope_tables",
    )(pos, freq_lane)


_COL_A = 0
_COL_Q = 2 * CONV_CH
_COL_K = _COL_Q + MIX_WIDTH
_COL_V = _COL_K + MIX_WIDTH
_COL_HQ = _COL_V + MIX_WIDTH
_COL_HF = _COL_HQ + MIX_WIDTH
_COL_HI = _COL_HF + MIX_WIDTH
_COL_HG = _COL_HI + MIX_WIDTH
_COL_GATE = _COL_HG + MIX_WIDTH


def _inproj_kernel(x_ref, w_ref, c_ref, sa_ref, sb_ref,
                   a_ref, q_ref, k_ref, v_ref, hq_ref, hf_ref, hi_ref, hg_ref, gt_ref):
    xb = x_ref[...].astype(BF16)

    def mm(c0, width):
        return jnp.dot(xb, w_ref[:, c0:c0 + width], preferred_element_type=F32)

    def rotary(t, scale):
        c, sa, sb = c_ref[...], sa_ref[...], sb_ref[...]
        outs = []
        for h in range(DA_HEADS):
            th = t[:, h * LANES:(h + 1) * LANES]
            r = th * c + pltpu.roll(th, LANES - ROT_DIM // 2, 1) * sa + pltpu.roll(th, ROT_DIM // 2, 1) * sb
            outs.append(r * scale)
        return jnp.concatenate(outs, axis=1)

    for j in range(2):
        a_ref[:, j * CONV_CH:(j + 1) * CONV_CH] = mm(_COL_A + j * CONV_CH, CONV_CH).astype(BF16)
    q_ref[...] = rotary(mm(_COL_Q, MIX_WIDTH), DA_SUB ** -0.5).astype(BF16)
    k_ref[...] = rotary(mm(_COL_K, MIX_WIDTH), 1.0).astype(BF16)
    v_ref[...] = mm(_COL_V, MIX_WIDTH).astype(BF16)
    hq_ref[...] = mm(_COL_HQ, MIX_WIDTH).astype(BF16)
    hf_ref[...] = mm(_COL_HF, MIX_WIDTH).astype(BF16)
    hi_ref[...] = mm(_COL_HI, MIX_WIDTH).astype(BF16)
    hg_ref[...] = mm(_COL_HG, MIX_WIDTH).astype(BF16)
    for j in range(N_BRANCH * D_MODEL // MIX_WIDTH):
        gt_ref[:, j * MIX_WIDTH:(j + 1) * MIX_WIDTH] = mm(_COL_GATE + j * MIX_WIDTH, MIX_WIDTH).astype(BF16)


def _in_projection(x2, w_in_bf16, rope):
    T = x2.shape[0]
    tm = min(T, 512)
    widths = [2 * CONV_CH] + [MIX_WIDTH] * 7 + [N_BRANCH * D_MODEL]
    row = lambda i: (i, 0)
    return pl.pallas_call(
        _inproj_kernel,
        grid=(T // tm,),
        in_specs=[pl.BlockSpec((tm, D_MODEL), row),
                  pl.BlockSpec((D_MODEL, IN_COLS), lambda i: (0, 0), pipeline_mode=pl.Buffered(1)),
                  pl.BlockSpec((tm, LANES), row),
                  pl.BlockSpec((tm, LANES), row),
                  pl.BlockSpec((tm, LANES), row)],
        out_specs=[pl.BlockSpec((tm, w), row) for w in widths],
        out_shape=[jax.ShapeDtypeStruct((T, w), BF16) for w in widths],
        compiler_params=_cparams(("parallel",)),
        name="in_projection",
    )(x2, w_in_bf16, *rope)


_HALO = 32
_CONV_ROWS = 32


def _conv_kernel(a_ref, halo_ref, w_ref, cb_ref, g_ref, b_ref, o_ref, ush_ref, y_ref, *, ts):
    j = pl.program_id(1)

    def glu(a):
        a = a.astype(F32)
        return a[:, :CONV_CH] * _sigmoid(a[:, CONV_CH:])

    u_cur = glu(a_ref[...])
    u_halo = jnp.where(j > 0, glu(halo_ref[...]), 0.0)
    u_ext = jnp.concatenate([u_halo, u_cur], axis=0)
    n_al = ts + _HALO - SUBLANES
    ush_ref[0] = u_ext
    for s in range(1, SUBLANES):
        ush_ref[s, 0:n_al, :] = u_ext[s:s + n_al, :]

    off0 = _HALO - (CONV_K - 1)

    def rows(r, carry):
        r0 = pl.multiple_of(r * _CONV_ROWS, _CONV_ROWS)
        for cb in range(CONV_CH // LANES):
            cs = slice(cb * LANES, (cb + 1) * LANES)
            acc = jnp.zeros((_CONV_ROWS, LANES), F32)
            for k in range(CONV_K):
                off = off0 + k
                s, al = off % SUBLANES, (off // SUBLANES) * SUBLANES
                acc = acc + w_ref[k:k + 1, cs] * ush_ref[s, pl.ds(r0 + al, _CONV_ROWS), cs]
            y_ref[pl.ds(r0, _CONV_ROWS), cs] = acc
        return carry

    lax.fori_loop(0, ts // _CONV_ROWS, rows, 0)
    y = y_ref[...] + cb_ref[...]
    yn = _layer_norm_rows(y, g_ref[...], b_ref[...])
    o_ref[...] = (yn * _sigmoid(yn)).astype(BF16)


def _conformer_conv(a_in, conv_w, conv_b, ln_g, ln_b, B, S):
    T = B * S
    ts = min(S, 512)
    nt = S // ts
    vec = lambda v: v.reshape(1, CONV_CH)
    return pl.pallas_call(
        functools.partial(_conv_kernel, ts=ts),
        grid=(B, nt),
        in_specs=[pl.BlockSpec((ts, 2 * CONV_CH), lambda b, j: (b * nt + j, 0)),
                  pl.BlockSpec((_HALO, 2 * CONV_CH),
                               lambda b, j: (jnp.maximum((b * nt + j) * (ts // _HALO) - 1, 0), 0)),
                  pl.BlockSpec((CONV_K, CONV_CH), lambda b, j: (0, 0)),
                  pl.BlockSpec((1, CONV_CH), lambda b, j: (0, 0)),
                  pl.BlockSpec((1, CONV_CH), lambda b, j: (0, 0)),
                  pl.BlockSpec((1, CONV_CH), lambda b, j: (0, 0))],
        out_specs=pl.BlockSpec((ts, CONV_CH), lambda b, j: (b * nt + j, 0)),
        out_shape=jax.ShapeDtypeStruct((T, CONV_CH), BF16),
        scratch_shapes=[pltpu.VMEM((SUBLANES, ts + _HALO, CONV_CH), F32),
                        pltpu.VMEM((ts, CONV_CH), F32)],
        compiler_params=_cparams(("parallel", "parallel")),
        name="conformer_conv",
    )(a_in, a_in, conv_w, vec(conv_b), vec(ln_g), vec(ln_b))


def _attn_kernel(q_ref, k_ref, v_ref, lam_ref, g_ref, o_ref, *, tq, tk, lam_init):
    qi = pl.program_id(2)
    q = q_ref[...]
    lane = lax.broadcasted_iota(jnp.int32, q.shape, 1)
    zero = jnp.zeros_like(q)
    qq = jnp.concatenate([jnp.where(lane < DA_SUB, q, zero), jnp.where(lane >= DA_SUB, q, zero)], axis=0)
    r_in = lax.broadcasted_iota(jnp.int32, (tq, 1), 0)
    row_pos = qi * tq + jnp.concatenate([r_in, r_in], axis=0)

    def step(j, carry):
        m, l, acc = carry
        k0 = pl.multiple_of(j * tk, tk)
        kb = k_ref[pl.ds(k0, tk), :]
        vb = v_ref[pl.ds(k0, tk), :]
        s = lax.dot_general(qq, kb, (((1,), (1,)), ((), ())), preferred_element_type=F32)
        col = k0 + lax.broadcasted_iota(jnp.int32, (1, tk), 1)
        s = jnp.where(col <= row_pos, s, NEG_BIG)
        m_new = jnp.maximum(m, jnp.max(s, axis=1, keepdims=True))
        alpha = jnp.exp(m - m_new)
        p = jnp.exp(s - m_new)
        l = alpha * l + jnp.sum(p, axis=1, keepdims=True)
        acc = alpha * acc + jnp.dot(p.astype(BF16), vb, preferred_element_type=F32)
        return m_new, l, acc

    n_kv = (qi * tq + tq + tk - 1) // tk
    init = (jnp.full((2 * tq, 1), NEG_BIG, F32), jnp.zeros((2 * tq, 1), F32),
            jnp.zeros((2 * tq, LANES), F32))
    _, l, acc = lax.fori_loop(0, n_kv, step, init)
    o12 = acc / l
    lp = lam_ref[...]
    lam = (jnp.exp(jnp.sum(lp[0:1] * lp[1:2], axis=1, keepdims=True))
           - jnp.exp(jnp.sum(lp[2:3] * lp[3:4], axis=1, keepdims=True)) + lam_init)
    o = o12[:tq] - lam * o12[tq:]
    ms = jnp.mean(o * o, axis=1, keepdims=True)
    o_ref[...] = (o * lax.rsqrt(ms + LN_EPS) * g_ref[...] * (1.0 - lam_init)).astype(BF16)


def _diff_attention(q, k, v, lam_params, norm_g, layer_idx, B, S):
    T = B * S
    tq = min(S, 256)
    tk = min(S, 256)
    nq = S // tq
    lam_init = 0.8 - 0.6 * math.exp(-0.3 * layer_idx)
    return pl.pallas_call(
        functools.partial(_attn_kernel, tq=tq, tk=tk, lam_init=lam_init),
        grid=(B, DA_HEADS, nq),
        in_specs=[pl.BlockSpec((tq, LANES), lambda b, h, i: (b * nq + i, h)),
                  pl.BlockSpec((S, LANES), lambda b, h, i: (b, h)),
                  pl.BlockSpec((S, LANES), lambda b, h, i: (b, h)),
                  pl.BlockSpec((4, DA_SUB), lambda b, h, i: (0, 0)),
                  pl.BlockSpec((1, LANES), lambda b, h, i: (0, 0))],
        out_specs=pl.BlockSpec((tq, LANES), lambda b, h, i: (b * nq + i, h)),
        out_shape=jax.ShapeDtypeStruct((T, MIX_WIDTH), BF16),
        compiler_params=_cparams(("parallel", "parallel", "arbitrary")),
        name="diff_attention",
    )(q, k, v, lam_params, norm_g.reshape(1, LANES))


_HG_CHUNK = 64


def _hgrn_kernel(q_ref, f_ref, i_ref, g_ref, lb_ref, ng_ref, o_ref, st_ref, *, n_chunks):
    C = _HG_CHUNK

    @pl.when(pl.program_id(1) == 0)
    def _():
        st_ref[...] = jnp.zeros_like(st_ref)

    rr = lax.broadcasted_iota(jnp.int32, (C, C), 0)
    cc = lax.broadcasted_iota(jnp.int32, (C, C), 1)
    causal = rr >= cc
    tri = jnp.where(causal, 1.0, 0.0).astype(BF16)

    for ci in range(n_chunks):
        rows = slice(ci * C, (ci + 1) * C)
        for h in range(HG_HEADS):
            cols = slice(h * HG_DK, (h + 1) * HG_DK)
            z = f_ref[rows, cols].astype(F32)
            lb = lb_ref[:, cols]
            e = jnp.exp(-jnp.abs(z))
            log_sig = jnp.minimum(z, 0.0) - jnp.log(1.0 + e)
            ya = jnp.log(lb)
            yb = jnp.log(1.0 - lb) + log_sig
            log_f = jnp.maximum(ya, yb) + jnp.log(1.0 + jnp.exp(-jnp.abs(ya - yb)))
            kf = (1.0 - lb) * jnp.where(z >= 0.0, e, 1.0) / (1.0 + e)
            hi = log_f.astype(BF16)
            r1 = log_f - hi.astype(F32)
            mid = r1.astype(BF16)
            lo = (r1 - mid.astype(F32)).astype(BF16)
            b3 = jnp.dot(tri, jnp.concatenate([hi, mid, lo], axis=1), preferred_element_type=F32)
            bcum = b3[:, 0:HG_DK] + b3[:, HG_DK:2 * HG_DK] + b3[:, 2 * HG_DK:3 * HG_DK]
            b_last = bcum[C - 1:C, :]
            mref = 0.5 * b_last
            qv = q_ref[rows, cols].astype(F32)
            qf = qv * _sigmoid(qv)
            qe = qf * jnp.exp(bcum)
            qd = (qe * jnp.exp(-mref)).astype(BF16)
            kd = kf * jnp.exp(mref - bcum)
            kdec = (kd * jnp.exp(mref)).astype(BF16)
            vv = i_ref[rows, cols]
            att = lax.dot_general(qd, kd.astype(BF16), (((1,), (1,)), ((), ())),
                                  preferred_element_type=F32)
            att = jnp.where(causal, att, 0.0).astype(BF16)
            st = st_ref[h]
            o = (lax.dot_general(qe.astype(BF16), st.astype(BF16), (((1,), (1,)), ((), ())),
                                 preferred_element_type=F32)
                 + jnp.dot(att, vv, preferred_element_type=F32))
            st_ref[h] = st * jnp.exp(b_last) + lax.dot_general(
                vv, kdec, (((0,), (0,)), ((), ())), preferred_element_type=F32)
            ms = jnp.mean(o * o, axis=1, keepdims=True)
            gv = g_ref[rows, cols].astype(F32)
            o_ref[rows, cols] = (o * lax.rsqrt(ms + LN_EPS) * ng_ref[...] * (gv * _sigmoid(gv))).astype(BF16)


def _hgrn2(hq, hf, hi, hg, lb, norm_g, B, S):
    T = B * S
    ts = min(S, 256)
    nt = S // ts
    blk = pl.BlockSpec((ts, MIX_WIDTH), lambda b, j: (b * nt + j, 0))
    return pl.pallas_call(
        functools.partial(_hgrn_kernel, n_chunks=ts // _HG_CHUNK),
        grid=(B, nt),
        in_specs=[blk, blk, blk, blk,
                  pl.BlockSpec((1, MIX_WIDTH), lambda b, j: (0, 0)),
                  pl.BlockSpec((1, HG_DK), lambda b, j: (0, 0))],
        out_specs=blk,
        out_shape=jax.ShapeDtypeStruct((T, MIX_WIDTH), BF16),
        scratch_shapes=[pltpu.VMEM((HG_HEADS, HG_DK, HG_DK), F32)],
        compiler_params=_cparams(("parallel", "arbitrary")),
        name="hgrn2",
    )(hq, hf, hi, hg, lb.reshape(1, MIX_WIDTH), norm_g.reshape(1, HG_DK))


_ROUTE_COLS = N_GROUPS + N_EXPERTS


def _merge_kernel(x_ref, ya_ref, yb_ref, yc_ref, gt_ref, wbr_ref, bg_ref, wout_ref, lng_ref, lnb_ref,
                  wr_ref, br_ref, x1_ref, eid_ref, wts_ref):
    mix = None
    for n, y_ref in enumerate((ya_ref, yb_ref, yc_ref)):
        proj = jnp.dot(y_ref[...], wbr_ref[n], preferred_element_type=F32)
        gate = _sigmoid(gt_ref[:, n * D_MODEL:(n + 1) * D_MODEL].astype(F32) + bg_ref[n:n + 1, :])
        mix = gate * proj if mix is None else mix + gate * proj
    mo = jnp.dot(mix.astype(BF16), wout_ref[...], preferred_element_type=F32)
    x1 = _layer_norm_rows(DN_ALPHA * x_ref[...] + mo, lng_ref[...], lnb_ref[...])
    x1_ref[...] = x1

    logits = jnp.dot(x1, wr_ref[...], preferred_element_type=F32,
                     precision=lax.Precision.HIGHEST) + br_ref[...]
    lane = lax.broadcasted_iota(jnp.int32, logits.shape, 1)
    big = jnp.int32(1 << 20)
    is_g = lane < N_GROUPS
    gl = jnp.where(is_g, logits, NEG_BIG)
    gmax = jnp.max(gl, axis=1, keepdims=True)
    g_idx = jnp.min(jnp.where(is_g & (gl == gmax), lane, big), axis=1, keepdims=True)
    p_group = 1.0 / jnp.sum(jnp.where(is_g, jnp.exp(gl - gmax), 0.0), axis=1, keepdims=True)
    lo = N_GROUPS + EXPERTS_PER_GROUP * g_idx
    sel = (lane >= lo) & (lane < lo + EXPERTS_PER_GROUP)
    el = jnp.where(sel, logits, NEG_BIG)
    v1 = jnp.max(el, axis=1, keepdims=True)
    i1 = jnp.min(jnp.where(sel & (el == v1), lane, big), axis=1, keepdims=True)
    sel2 = sel & (lane != i1)
    el2 = jnp.where(sel2, logits, NEG_BIG)
    v2 = jnp.max(el2, axis=1, keepdims=True)
    i2 = jnp.min(jnp.where(sel2 & (el2 == v2), lane, big), axis=1, keepdims=True)
    e2 = jnp.exp(v2 - v1)
    w1 = p_group / (1.0 + e2)
    w2 = p_group * e2 / (1.0 + e2)
    eid_ref[...] = jnp.where(lane == 0, i1 - N_GROUPS, jnp.where(lane == 1, i2 - N_GROUPS, 0))
    wts_ref[...] = jnp.where(lane == 0, w1, jnp.where(lane == 1, w2, 0.0))


def _merge(x2, ya, yb, yc, gates, w_branch, b_gate, w_out, ln_g, ln_b, w_route, b_route):
    T = x2.shape[0]
    tm = min(T, 512)
    row = lambda i: (i, 0)
    const2 = lambda i: (0, 0)
    return pl.pallas_call(
        _merge_kernel,
        grid=(T // tm,),
        in_specs=[pl.BlockSpec((tm, D_MODEL), row),
                  pl.BlockSpec((tm, MIX_WIDTH), row),
                  pl.BlockSpec((tm, MIX_WIDTH), row),
                  pl.BlockSpec((tm, MIX_WIDTH), row),
                  pl.BlockSpec((tm, N_BRANCH * D_MODEL), row),
                  pl.BlockSpec((N_BRANCH, MIX_WIDTH, D_MODEL), lambda i: (0, 0, 0)),
                  pl.BlockSpec((N_BRANCH, D_MODEL), const2),
                  pl.BlockSpec((D_MODEL, D_MODEL), const2),
                  pl.BlockSpec((1, D_MODEL), const2),
                  pl.BlockSpec((1, D_MODEL), const2),
                  pl.BlockSpec((D_MODEL, LANES), const2),
                  pl.BlockSpec((1, LANES), const2)],
        out_specs=[pl.BlockSpec((tm, D_MODEL), row), pl.BlockSpec((tm, LANES), row),
                   pl.BlockSpec((tm, LANES), row)],
        out_shape=[jax.ShapeDtypeStruct((T, D_MODEL), F32), jax.ShapeDtypeStruct((T, LANES), jnp.int32),
                   jax.ShapeDtypeStruct((T, LANES), F32)],
        compiler_params=_cparams(("parallel",)),
        name="merge_router",
    )(x2, ya, yb, yc, gates, w_branch, b_gate, w_out, ln_g.reshape(1, -1), ln_b.reshape(1, -1),
      w_route, b_route)


_TOK_TILE = 256
_FFN_TILE = 512


def _rank_kernel(eid_ref, rt_ref, cnt_ref, base_ref):
    tm = eid_ref.shape[0]

    @pl.when(pl.program_id(0) == 0)
    def _():
        base_ref[...] = jnp.zeros_like(base_ref)

    lane = lax.broadcasted_iota(jnp.int32, (tm, LANES), 1)
    e1 = eid_ref[:, 0:1]
    e2 = eid_ref[:, 1:2]
    oh1 = lane == e1
    oh2 = lane == e2
    oh = jnp.where(oh1 | oh2, 1.0, 0.0)
    rr = lax.broadcasted_iota(jnp.int32, (tm, tm), 0)
    cc = lax.broadcasted_iota(jnp.int32, (tm, tm), 1)
    before = jnp.where(rr > cc, 1.0, 0.0).astype(BF16)
    pref = jnp.dot(before, oh.astype(BF16), preferred_element_type=F32) + base_ref[...]
    r1 = jnp.sum(jnp.where(oh1, pref, 0.0), axis=1, keepdims=True)
    r2 = jnp.sum(jnp.where(oh2, pref, 0.0), axis=1, keepdims=True)
    slab = jnp.where(lane == 0, e1.astype(F32),
                     jnp.where(lane == 1, e2.astype(F32),
                               jnp.where(lane == 2, r1, jnp.where(lane == 3, r2, 0.0))))
    rt_ref[0] = slab.T[0:SUBLANES, :].astype(jnp.int32)
    base_ref[...] += jnp.sum(oh, axis=0, keepdims=True)
    cnt_ref[...] = base_ref[...]


def _route_ranks(eid):
    T = eid.shape[0]
    tm = min(T, _TOK_TILE)
    nt = T // tm
    return pl.pallas_call(
        _rank_kernel,
        grid=(nt,),
        in_specs=[pl.BlockSpec((tm, LANES), lambda i: (i, 0))],
        out_specs=[pl.BlockSpec((1, SUBLANES, tm), lambda i: (i, 0, 0)),
                   pl.BlockSpec((1, LANES), lambda i: (0, 0))],
        out_shape=[jax.ShapeDtypeStruct((nt, SUBLANES, tm), jnp.int32),
                   jax.ShapeDtypeStruct((1, LANES), F32)],
        scratch_shapes=[pltpu.VMEM((1, LANES), F32)],
        compiler_params=_cparams(("arbitrary",)),
        name="route_ranks",
    )(eid)


def _dest_kernel(off_ref, rt_ref, d_ref):
    e = rt_ref[:, 0:2, :]
    dest = rt_ref[:, 2:4, :]
    for k in range(N_EXPERTS):
        dest = dest + jnp.where(e == k, off_ref[k], 0)
    d_ref[...] = dest


def _route_dests(offsets, rt):
    nt, _, tm = rt.shape
    return pl.pallas_call(
        _dest_kernel,
        grid_spec=pltpu.PrefetchScalarGridSpec(
            num_scalar_prefetch=1, grid=(1,),
            in_specs=[pl.BlockSpec((nt, SUBLANES, tm), lambda i, off: (0, 0, 0))],
            out_specs=pl.BlockSpec((nt, 2, tm), lambda i, off: (0, 0, 0))),
        out_shape=jax.ShapeDtypeStruct((nt, 2, tm), jnp.int32),
        compiler_params=_cparams(("arbitrary",)),
        name="route_dests",
    )(offsets, rt)


def _row_copy(src, dst, sem):
    return pltpu.make_async_copy(src, dst, sem)


def _scatter_kernel(tails_ref, dest_ref, x1_ref, xs_ref, xbuf_ref, zero_ref, sems):
    i = pl.program_id(0)
    nt = pl.num_programs(0)
    tm = x1_ref.shape[0]
    slot = i % 2

    def zero_copy(e):
        start = pl.multiple_of(tails_ref[e] // SUBLANES * SUBLANES, SUBLANES)
        return _row_copy(zero_ref, xs_ref.at[pl.ds(start, _FFN_TILE)], sems.at[2])

    @pl.when(i == 0)
    def _():
        zero_ref[...] = jnp.zeros_like(zero_ref)
        for e in range(N_EXPERTS):
            zero_copy(e).start()
        for e in range(N_EXPERTS):
            zero_copy(e).wait()

    xbuf_ref[slot] = x1_ref[...]

    def issue(t, c):
        for k in range(2):
            d = dest_ref[0, k, t]
            _row_copy(xbuf_ref.at[slot, pl.ds(t, 1)], xs_ref.at[pl.ds(d, 1)], sems.at[slot]).start()
        return c

    lax.fori_loop(0, tm, issue, 0, unroll=8)

    def drain(s):
        def w(t, c):
            _row_copy(xbuf_ref.at[s, pl.ds(0, 1)], xs_ref.at[pl.ds(0, 1)], sems.at[s]).wait()
            return c
        lax.fori_loop(0, 2 * tm, w, 0)

    @pl.when(i > 0)
    def _():
        drain(1 - slot)

    @pl.when(i == nt - 1)
    def _():
        drain(slot)


def _scatter_rows(tails, dest, x1, n_rows):
    nt, _, tm = dest.shape
    return pl.pallas_call(
        _scatter_kernel,
        grid_spec=pltpu.PrefetchScalarGridSpec(
            num_scalar_prefetch=1, grid=(nt,),
            in_specs=[pl.BlockSpec((1, 2, tm), lambda i, tl: (i, 0, 0), memory_space=pltpu.SMEM),
                      pl.BlockSpec((tm, D_MODEL), lambda i, tl: (i, 0))],
            out_specs=pl.BlockSpec(memory_space=pl.ANY),
            scratch_shapes=[pltpu.VMEM((2, tm, D_MODEL), F32),
                            pltpu.VMEM((_FFN_TILE, D_MODEL), F32),
                            pltpu.SemaphoreType.DMA((3,))]),
        out_shape=jax.ShapeDtypeStruct((n_rows, D_MODEL), F32),
        compiler_params=_cparams(("arbitrary",)),
        name="moe_scatter",
    )(tails, dest, x1)


def _ffn_kernel(te_ref, nu_ref, xs_ref, w1_ref, w3_ref, w2_ref, ys_ref):
    @pl.when(pl.program_id(0) < nu_ref[0])
    def _():
        xb = xs_ref[...].astype(BF16)
        h1 = jnp.dot(xb, w1_ref[0], preferred_element_type=F32)
        h3 = jnp.dot(xb, w3_ref[0], preferred_element_type=F32)
        hh = (h1 * _sigmoid(h1) * h3).astype(BF16)
        ys_ref[...] = jnp.dot(hh, w2_ref[0], preferred_element_type=F32)


def _expert_ffn(tile_expert, n_used, xs, w1, w3, w2):
    n_rows = xs.shape[0]
    n_tiles = n_rows // _FFN_TILE
    rows = lambda i, te, nu: (jnp.minimum(i, nu[0] - 1), 0)
    wsel = lambda i, te, nu: (te[jnp.minimum(i, n_tiles - 1)], 0, 0)
    return pl.pallas_call(
        _ffn_kernel,
        grid_spec=pltpu.PrefetchScalarGridSpec(
            num_scalar_prefetch=2, grid=(n_tiles,),
            in_specs=[pl.BlockSpec((_FFN_TILE, D_MODEL), rows),
                      pl.BlockSpec((1, D_MODEL, D_EXPERT), wsel),
                      pl.BlockSpec((1, D_MODEL, D_EXPERT), wsel),
                      pl.BlockSpec((1, D_EXPERT, D_MODEL), wsel)],
            out_specs=pl.BlockSpec((_FFN_TILE, D_MODEL), rows)),
        out_shape=jax.ShapeDtypeStruct((n_rows, D_MODEL), F32),
        compiler_params=_cparams(("arbitrary",)),
        name="moe_expert_ffn",
    )(tile_expert, n_used, xs, w1, w3, w2)


def _combine_kernel(dcur_ref, dnext_ref, x1_ref, wts_ref, ys_ref, lng_ref, lnb_ref, o_ref, ybuf_ref, sems):
    i = pl.program_id(0)
    nt = pl.num_programs(0)
    tm = x1_ref.shape[0]
    slot = i % 2

    def issue(d_ref, s):
        def body(t, c):
            for k in range(2):
                d = d_ref[0, k, t]
                _row_copy(ys_ref.at[pl.ds(d, 1)], ybuf_ref.at[s, k, pl.ds(t, 1)], sems.at[s]).start()
            return c
        lax.fori_loop(0, tm, body, 0, unroll=8)

    @pl.when(i == 0)
    def _():
        issue(dcur_ref, 0)

    @pl.when(i + 1 < nt)
    def _():
        issue(dnext_ref, 1 - slot)

    def w(t, c):
        _row_copy(ys_ref.at[pl.ds(0, 1)], ybuf_ref.at[slot, 0, pl.ds(0, 1)], sems.at[slot]).wait()
        return c
    lax.fori_loop(0, 2 * tm, w, 0)

    w1 = wts_ref[:, 0:1]
    w2 = wts_ref[:, 1:2]
    moe = w1 * ybuf_ref[slot, 0] + w2 * ybuf_ref[slot, 1]
    o_ref[...] = _layer_norm_rows(DN_ALPHA * x1_ref[...] + moe, lng_ref[...], lnb_ref[...])


def _combine(dest, x1, wts, ys, ln_g, ln_b):
    T = x1.shape[0]
    nt, _, tm = dest.shape
    dspec = lambda f: pl.BlockSpec((1, 2, tm), f, memory_space=pltpu.SMEM)
    return pl.pallas_call(
        _combine_kernel,
        grid=(nt,),
        in_specs=[dspec(lambda i: (i, 0, 0)),
                  dspec(lambda i: (jnp.minimum(i + 1, nt - 1), 0, 0)),
                  pl.BlockSpec((tm, D_MODEL), lambda i: (i, 0)),
                  pl.BlockSpec((tm, LANES), lambda i: (i, 0)),
                  pl.BlockSpec(memory_space=pl.ANY),
                  pl.BlockSpec((1, D_MODEL), lambda i: (0, 0)),
                  pl.BlockSpec((1, D_MODEL), lambda i: (0, 0))],
        out_specs=pl.BlockSpec((tm, D_MODEL), lambda i: (i, 0)),
        out_shape=jax.ShapeDtypeStruct((T, D_MODEL), F32),
        scratch_shapes=[pltpu.VMEM((2, 2, tm, D_MODEL), F32), pltpu.SemaphoreType.DMA((2,))],
        compiler_params=_cparams(("arbitrary",)),
        name="moe_combine",
    )(dest, dest, x1, wts, ys, ln_g.reshape(1, -1), ln_b.reshape(1, -1))


def _moe(x1, eid, wts, w1, w3, w2, ln_g, ln_b):
    T = x1.shape[0]
    rt, counts = _route_ranks(eid)
    cnt = counts[0, :N_EXPERTS].astype(jnp.int32)
    padded = (cnt + _FFN_TILE - 1) // _FFN_TILE * _FFN_TILE
    ends = jnp.cumsum(padded)
    offsets = ends - padded
    n_rows = 2 * T + (N_EXPERTS + 1) * _FFN_TILE
    n_tiles = n_rows // _FFN_TILE
    tile_start = jnp.arange(n_tiles, dtype=jnp.int32) * _FFN_TILE
    tile_expert = jnp.minimum(jnp.sum(tile_start[:, None] >= ends[None, :], axis=1), N_EXPERTS - 1)
    n_used = (ends[-1:] // _FFN_TILE).astype(jnp.int32)
    dest = _route_dests(offsets.astype(jnp.int32), rt)
    xs = _scatter_rows((offsets + cnt).astype(jnp.int32), dest, x1, n_rows)
    ys = _expert_ffn(tile_expert.astype(jnp.int32), n_used, xs, w1, w3, w2)
    return _combine(dest, x1, wts, ys, ln_g, ln_b)


def _router_weights(router_group, router_group_b, router_expert, router_expert_b):
    we = jnp.transpose(router_expert, (1, 0, 2)).reshape(D_MODEL, N_EXPERTS)
    w = jnp.concatenate([router_group, we], axis=1)
    b = jnp.concatenate([router_group_b, router_expert_b.reshape(N_EXPERTS)])
    pad = LANES - _ROUTE_COLS
    return jnp.pad(w, ((0, 0), (0, pad))), jnp.pad(b, (0, pad)).reshape(1, LANES)


def kernel(x, positions, w_in, conv_w, conv_b, conv_ln_g, conv_ln_b, da_lambda, da_norm_g, hg_lb, hg_norm_g, w_branch, b_gate, w_out, ln1_g, ln1_b, router_group, router_group_b, router_expert, router_expert_b, exp_w1, exp_w3, exp_w2, ln2_g, ln2_b):
    B, S, D = x.shape
    T = B * S
    rope = _rope_tables(positions)
    lb_tab = jnp.cumsum(jax.nn.softmax(hg_lb.astype(F32), axis=0), axis=0)
    lb_tab = lb_tab - lb_tab[0:1]
    x2 = x.reshape(T, D)
    for l in range(DEPTH):
        a_in, q, k, v, hq, hf, hi, hg, gates = _in_projection(x2, w_in[l].astype(BF16), rope)
        y_a = _conformer_conv(a_in, conv_w[l], conv_b[l], conv_ln_g[l], conv_ln_b[l], B, S)
        y_b = _diff_attention(q, k, v, da_lambda[l], da_norm_g[l], l, B, S)
        y_c = _hgrn2(hq, hf, hi, hg, lb_tab[l], hg_norm_g[l], B, S)
        w_route, b_route = _router_weights(router_group[l], router_group_b[l],
                                           router_expert[l], router_expert_b[l])
        x1, eid, wts = _merge(x2, y_a, y_b, y_c, gates, w_branch[l].astype(BF16), b_gate[l],
                          w_out[l].astype(BF16), ln1_g[l], ln1_b[l], w_route, b_route)
        x2 = _moe(x1, eid, wts, exp_w1[l].astype(BF16), exp_w3[l].astype(BF16), exp_w2[l].astype(BF16),
                  ln2_g[l], ln2_b[l])
    return x2.reshape(B, S, D)
```

```python
import functools
import math

import jax
import jax.numpy as jnp
from jax import lax
from jax.experimental import pallas as pl
from jax.experimental.pallas import tpu as pltpu

F32 = jnp.float32
BF16 = jnp.bfloat16

D_MODEL = 1024
DEPTH = 2
CONV_CH = 512
CONV_K = 31
DA_HEADS = 4
DA_SUB = 64
ROT_DIM = DA_SUB // 4
ROPE_THETA = 500000.0
HG_HEADS = 4
HG_DK = 128
MIX_WIDTH = 512
N_BRANCH = 3
N_GROUPS = 4
EXPERTS_PER_GROUP = 8
N_EXPERTS = N_GROUPS * EXPERTS_PER_GROUP
D_EXPERT = 512
DN_ALPHA = (2 * DEPTH) ** 0.25
LN_EPS = 1e-5
IN_COLS = 2 * CONV_CH + 7 * MIX_WIDTH + N_BRANCH * D_MODEL

LANES = 128
SUBLANES = 8
VMEM_LIMIT_BYTES = 56 * 1024 * 1024

NEG_BIG = -1e30


def _cparams(sem):
    return pltpu.CompilerParams(dimension_semantics=sem, vmem_limit_bytes=VMEM_LIMIT_BYTES)


def _layer_norm_rows(r, g, b):
    mu = jnp.mean(r, axis=-1, keepdims=True)
    d = r - mu
    var = jnp.mean(d * d, axis=-1, keepdims=True)
    return d * lax.rsqrt(var + LN_EPS) * g + b


def _sigmoid(x):
    return 1.0 / (1.0 + jnp.exp(-x))


def _rope_kernel(pos_ref, freq_ref, c_ref, sa_ref, sb_ref):
    ang = pos_ref[...].astype(F32) * freq_ref[...]
    lane = lax.broadcasted_iota(jnp.int32, ang.shape, 1) & (DA_SUB - 1)
    half = ROT_DIM // 2
    cs = jnp.cos(ang)
    sn = jnp.sin(ang)
    c_ref[...] = jnp.where(lane < ROT_DIM, cs, 1.0)
    sa_ref[...] = jnp.where(lane < half, -sn, 0.0)
    sb_ref[...] = jnp.where((lane >= half) & (lane < ROT_DIM), sn, 0.0)


def _rope_tables(positions):
    T = positions.size
    tm = min(T, 1024)
    inv_freq = ROPE_THETA ** (-jnp.arange(0, ROT_DIM, 2, dtype=F32) / ROT_DIM)
    freq_lane = jnp.tile(inv_freq, LANES // (ROT_DIM // 2))[None, :]
    pos = positions.reshape(T, 1)
    out = jax.ShapeDtypeStruct((T, LANES), F32)
    return pl.pallas_call(
        _rope_kernel,
        grid=(T // tm,),
        in_specs=[pl.BlockSpec((tm, 1), lambda i: (i, 0)),
                  pl.BlockSpec((1, LANES), lambda i: (0, 0))],
        out_specs=[pl.BlockSpec((tm, LANES), lambda i: (i, 0))] * 3,
        out_shape=[out, out, out],
        compiler_params=_cparams(("parallel",)),
        name="rope_tables",
    )(pos, freq_lane)


_COL_A = 0
_COL_Q = 2 * CONV_CH
_COL_K = _COL_Q + MIX_WIDTH
_COL_V = _COL_K + MIX_WIDTH
_COL_HQ = _COL_V + MIX_WIDTH
_COL_HF = _COL_HQ + MIX_WIDTH
_COL_HI = _COL_HF + MIX_WIDTH
_COL_HG = _COL_HI + MIX_WIDTH
_COL_GATE = _COL_HG + MIX_WIDTH


def _inproj_kernel(x_ref, w_ref, c_ref, sa_ref, sb_ref,
                   a_ref, q_ref, k_ref, v_ref, hq_ref, hf_ref, hi_ref, hg_ref, gt_ref):
    xb = x_ref[...].astype(BF16)

    def mm(c0, width):
        return jnp.dot(xb, w_ref[:, c0:c0 + width], preferred_element_type=F32)

    def rotary(t, scale):
        c, sa, sb = c_ref[...], sa_ref[...], sb_ref[...]
        outs = []
        for h in range(DA_HEADS):
            th = t[:, h * LANES:(h + 1) * LANES]
            r = th * c + pltpu.roll(th, LANES - ROT_DIM // 2, 1) * sa + pltpu.roll(th, ROT_DIM // 2, 1) * sb
            outs.append(r * scale)
        return jnp.concatenate(outs, axis=1)

    for j in range(2):
        a_ref[:, j * CONV_CH:(j + 1) * CONV_CH] = mm(_COL_A + j * CONV_CH, CONV_CH).astype(BF16)
    q_ref[...] = rotary(mm(_COL_Q, MIX_WIDTH), DA_SUB ** -0.5 * math.log2(math.e)).astype(BF16)
    k_ref[...] = rotary(mm(_COL_K, MIX_WIDTH), 1.0).astype(BF16)
    v_ref[...] = mm(_COL_V, MIX_WIDTH).astype(BF16)
    hq_ref[...] = mm(_COL_HQ, MIX_WIDTH).astype(BF16)
    hf_ref[...] = mm(_COL_HF, MIX_WIDTH).astype(BF16)
    hi_ref[...] = mm(_COL_HI, MIX_WIDTH).astype(BF16)
    hg_ref[...] = mm(_COL_HG, MIX_WIDTH).astype(BF16)
    for j in range(N_BRANCH * D_MODEL // MIX_WIDTH):
        gt_ref[:, j * MIX_WIDTH:(j + 1) * MIX_WIDTH] = mm(_COL_GATE + j * MIX_WIDTH, MIX_WIDTH).astype(BF16)


def _in_projection(x2, w_in_bf16, rope):
    T = x2.shape[0]
    tm = min(T, 512)
    widths = [2 * CONV_CH] + [MIX_WIDTH] * 7 + [N_BRANCH * D_MODEL]
    row = lambda i: (i, 0)
    return pl.pallas_call(
        _inproj_kernel,
        grid=(T // tm,),
        in_specs=[pl.BlockSpec((tm, D_MODEL), row),
                  pl.BlockSpec((D_MODEL, IN_COLS), lambda i: (0, 0), pipeline_mode=pl.Buffered(1)),
                  pl.BlockSpec((tm, LANES), row),
                  pl.BlockSpec((tm, LANES), row),
                  pl.BlockSpec((tm, LANES), row)],
        out_specs=[pl.BlockSpec((tm, w), row) for w in widths],
        out_shape=[jax.ShapeDtypeStruct((T, w), BF16) for w in widths],
        compiler_params=_cparams(("parallel",)),
        name="in_projection",
    )(x2, w_in_bf16, *rope)


_HALO = 32
_CONV_ROWS = 32


def _conv_kernel(a_ref, halo_ref, w_ref, cb_ref, g_ref, b_ref, o_ref, ush_ref, y_ref, *, ts):
    j = pl.program_id(1)

    def glu(a):
        a = a.astype(F32)
        return a[:, :CONV_CH] * _sigmoid(a[:, CONV_CH:])

    u_cur = glu(a_ref[...])
    u_halo = jnp.where(j > 0, glu(halo_ref[...]), 0.0)
    u_ext = jnp.concatenate([u_halo, u_cur], axis=0)
    n_al = ts + _HALO - SUBLANES
    ush_ref[0] = u_ext
    for s in range(1, SUBLANES):
        ush_ref[s, 0:n_al, :] = u_ext[s:s + n_al, :]

    off0 = _HALO - (CONV_K - 1)

    def rows(r, carry):
        r0 = pl.multiple_of(r * _CONV_ROWS, _CONV_ROWS)
        for cb in range(CONV_CH // LANES):
            cs = slice(cb * LANES, (cb + 1) * LANES)
            acc = jnp.zeros((_CONV_ROWS, LANES), F32)
            for k in range(CONV_K):
                off = off0 + k
                s, al = off % SUBLANES, (off // SUBLANES) * SUBLANES
                acc = acc + w_ref[k:k + 1, cs] * ush_ref[s, pl.ds(r0 + al, _CONV_ROWS), cs]
            y_ref[pl.ds(r0, _CONV_ROWS), cs] = acc
        return carry

    lax.fori_loop(0, ts // _CONV_ROWS, rows, 0)
    y = y_ref[...] + cb_ref[...]
    yn = _layer_norm_rows(y, g_ref[...], b_ref[...])
    o_ref[...] = (yn * _sigmoid(yn)).astype(BF16)


def _conformer_conv(a_in, conv_w, conv_b, ln_g, ln_b, B, S):
    T = B * S
    ts = min(S, 512)
    nt = S // ts
    vec = lambda v: v.reshape(1, CONV_CH)
    return pl.pallas_call(
        functools.partial(_conv_kernel, ts=ts),
        grid=(B, nt),
        in_specs=[pl.BlockSpec((ts, 2 * CONV_CH), lambda b, j: (b * nt + j, 0)),
                  pl.BlockSpec((_HALO, 2 * CONV_CH),
                               lambda b, j: (jnp.maximum((b * nt + j) * (ts // _HALO) - 1, 0), 0)),
                  pl.BlockSpec((CONV_K, CONV_CH), lambda b, j: (0, 0)),
                  pl.BlockSpec((1, CONV_CH), lambda b, j: (0, 0)),
                  pl.BlockSpec((1, CONV_CH), lambda b, j: (0, 0)),
                  pl.BlockSpec((1, CONV_CH), lambda b, j: (0, 0))],
        out_specs=pl.BlockSpec((ts, CONV_CH), lambda b, j: (b * nt + j, 0)),
        out_shape=jax.ShapeDtypeStruct((T, CONV_CH), BF16),
        scratch_shapes=[pltpu.VMEM((SUBLANES, ts + _HALO, CONV_CH), F32),
                        pltpu.VMEM((ts, CONV_CH), F32)],
        compiler_params=_cparams(("parallel", "parallel")),
        name="conformer_conv",
    )(a_in, a_in, conv_w, vec(conv_b), vec(ln_g), vec(ln_b))


def _attn_kernel(q_ref, k_ref, v_ref, lam_ref, g_ref, o_ref, vt_ref, *, tq, tk, lam_init):
    qi = pl.program_id(2)
    n_diag = tq // tk

    @pl.when(qi == 0)
    def _():
        vt_ref[...] = v_ref[...].astype(F32).T.astype(BF16)

    q = q_ref[...]
    lane = lax.broadcasted_iota(jnp.int32, q.shape, 1)
    zero = jnp.zeros_like(q)
    qq = jnp.concatenate([jnp.where(lane < DA_SUB, q, zero), jnp.where(lane >= DA_SUB, q, zero)], axis=0)

    def scores(j):
        k0 = pl.multiple_of(j * tk, tk)
        return lax.dot_general(k_ref[pl.ds(k0, tk), :], qq, (((1,), (1,)), ((), ())),
                               preferred_element_type=F32)

    def update(j, st, carry):
        m, l, acc = carry
        k0 = pl.multiple_of(j * tk, tk)
        m_new = jnp.maximum(m, jnp.max(st, axis=0, keepdims=True))
        alpha = jnp.exp2(m - m_new)
        p = jnp.exp2(st - m_new)
        l = alpha * l + jnp.sum(p, axis=0, keepdims=True)
        acc = alpha * acc + jnp.dot(vt_ref[:, pl.ds(k0, tk)], p.astype(BF16), preferred_element_type=F32)
        return m_new, l, acc

    def body(j, c):
        st_next = scores(j + 1)
        m, l, acc = update(j, c[3], c[0:3])
        return m, l, acc, st_next

    n_full = qi * n_diag
    init = (jnp.full((1, 2 * tq), NEG_BIG, F32), jnp.zeros((1, 2 * tq), F32),
            jnp.zeros((LANES, 2 * tq), F32), scores(0))
    m, l, acc, st = lax.fori_loop(0, n_full, body, init)
    q_in = lax.broadcasted_iota(jnp.int32, (1, tq), 1)
    qpos = jnp.concatenate([q_in, q_in], axis=1)
    for d in range(n_diag):
        st_next = scores(n_full + d + 1) if d + 1 < n_diag else None
        kpos = d * tk + lax.broadcasted_iota(jnp.int32, (tk, 1), 0)
        m, l, acc = update(n_full + d, jnp.where(kpos <= qpos, st, NEG_BIG), (m, l, acc))
        st = st_next
    o12 = acc / l
    lp = lam_ref[...]
    lam = (jnp.exp(jnp.sum(lp[0:1] * lp[1:2], axis=1, keepdims=True))
           - jnp.exp(jnp.sum(lp[2:3] * lp[3:4], axis=1, keepdims=True)) + lam_init)
    o = o12[:, :tq] - lam * o12[:, tq:]
    ms = jnp.mean(o * o, axis=0, keepdims=True)
    o = o * lax.rsqrt(ms + LN_EPS) * g_ref[...] * (1.0 - lam_init)
    o_ref[...] = o.T.astype(BF16)


def _diff_attention(q, k, v, lam_params, norm_g, layer_idx, B, S):
    T = B * S
    tq = min(S, 512)
    tk = min(S, 256)
    nq = S // tq
    lam_init = 0.8 - 0.6 * math.exp(-0.3 * layer_idx)
    return pl.pallas_call(
        functools.partial(_attn_kernel, tq=tq, tk=tk, lam_init=lam_init),
        grid=(B, DA_HEADS, nq),
        in_specs=[pl.BlockSpec((tq, LANES), lambda b, h, i: (b * nq + i, h)),
                  pl.BlockSpec((S, LANES), lambda b, h, i: (b, h)),
                  pl.BlockSpec((S, LANES), lambda b, h, i: (b, h)),
                  pl.BlockSpec((4, DA_SUB), lambda b, h, i: (0, 0)),
                  pl.BlockSpec((LANES, 1), lambda b, h, i: (0, 0))],
        out_specs=pl.BlockSpec((tq, LANES), lambda b, h, i: (b * nq + i, h)),
        out_shape=jax.ShapeDtypeStruct((T, MIX_WIDTH), BF16),
        scratch_shapes=[pltpu.VMEM((LANES, S), BF16)],
        compiler_params=_cparams(("parallel", "parallel", "arbitrary")),
        name="diff_attention",
    )(q, k, v, lam_params, norm_g.reshape(LANES, 1))


_HG_CHUNK = 64


def _hgrn_kernel(q_ref, f_ref, i_ref, g_ref, lb_ref, ng_ref, o_ref, st_ref, *, n_chunks):
    C = _HG_CHUNK

    @pl.when(pl.program_id(1) == 0)
    def _():
        st_ref[...] = jnp.zeros_like(st_ref)

    rr = lax.broadcasted_iota(jnp.int32, (C, C), 0)
    cc = lax.broadcasted_iota(jnp.int32, (C, C), 1)
    causal = rr >= cc
    tri = jnp.where(causal, 1.0, 0.0).astype(BF16)

    for ci in range(n_chunks):
        rows = slice(ci * C, (ci + 1) * C)
        for h in range(HG_HEADS):
            cols = slice(h * HG_DK, (h + 1) * HG_DK)
            z = f_ref[rows, cols].astype(F32)
            lb = lb_ref[:, cols]
            e = jnp.exp(-jnp.abs(z))
            log_sig = jnp.minimum(z, 0.0) - jnp.log(1.0 + e)
            ya = jnp.log(lb)
            yb = jnp.log(1.0 - lb) + log_sig
            log_f = jnp.maximum(ya, yb) + jnp.log(1.0 + jnp.exp(-jnp.abs(ya - yb)))
            kf = (1.0 - lb) * jnp.where(z >= 0.0, e, 1.0) / (1.0 + e)
            hi = log_f.astype(BF16)
            r1 = log_f - hi.astype(F32)
            mid = r1.astype(BF16)
            lo = (r1 - mid.astype(F32)).astype(BF16)
            b3 = jnp.dot(tri, jnp.concatenate([hi, mid, lo], axis=1), preferred_element_type=F32)
            bcum = b3[:, 0:HG_DK] + b3[:, HG_DK:2 * HG_DK] + b3[:, 2 * HG_DK:3 * HG_DK]
            b_last = bcum[C - 1:C, :]
            mref = 0.5 * b_last
            qv = q_ref[rows, cols].astype(F32)
            qf = qv * _sigmoid(qv)
            qe = qf * jnp.exp(bcum)
            qd = (qe * jnp.exp(-mref)).astype(BF16)
            kd = kf * jnp.exp(mref - bcum)
            kdec = (kd * jnp.exp(mref)).astype(BF16)
            vv = i_ref[rows, cols]
            att = lax.dot_general(qd, kd.astype(BF16), (((1,), (1,)), ((), ())),
                                  preferred_element_type=F32)
            att = jnp.where(causal, att, 0.0).astype(BF16)
            st = st_ref[h]
            o = (lax.dot_general(qe.astype(BF16), st.astype(BF16), (((1,), (1,)), ((), ())),
                                 preferred_element_type=F32)
                 + jnp.dot(att, vv, preferred_element_type=F32))
            st_ref[h] = st * jnp.exp(b_last) + lax.dot_general(
                vv, kdec, (((0,), (0,)), ((), ())), preferred_element_type=F32)
            ms = jnp.mean(o * o, axis=1, keepdims=True)
            gv = g_ref[rows, cols].astype(F32)
            o_ref[rows, cols] = (o * lax.rsqrt(ms + LN_EPS) * ng_ref[...] * (gv * _sigmoid(gv))).astype(BF16)


def _hgrn2(hq, hf, hi, hg, lb, norm_g, B, S):
    T = B * S
    ts = min(S, 256)
    nt = S // ts
    blk = pl.BlockSpec((ts, MIX_WIDTH), lambda b, j: (b * nt + j, 0))
    return pl.pallas_call(
        functools.partial(_hgrn_kernel, n_chunks=ts // _HG_CHUNK),
        grid=(B, nt),
        in_specs=[blk, blk, blk, blk,
                  pl.BlockSpec((1, MIX_WIDTH), lambda b, j: (0, 0)),
                  pl.BlockSpec((1, HG_DK), lambda b, j: (0, 0))],
        out_specs=blk,
        out_shape=jax.ShapeDtypeStruct((T, MIX_WIDTH), BF16),
        scratch_shapes=[pltpu.VMEM((HG_HEADS, HG_DK, HG_DK), F32)],
        compiler_params=_cparams(("parallel", "arbitrary")),
        name="hgrn2",
    )(hq, hf, hi, hg, lb.reshape(1, MIX_WIDTH), norm_g.reshape(1, HG_DK))


_ROUTE_COLS = N_GROUPS + N_EXPERTS


def _merge_kernel(x_ref, ya_ref, yb_ref, yc_ref, gt_ref, wbr_ref, bg_ref, wout_ref, lng_ref, lnb_ref,
                  wr_ref, br_ref, x1_ref, eid_ref, wts_ref):
    mix = None
    for n, y_ref in enumerate((ya_ref, yb_ref, yc_ref)):
        proj = jnp.dot(y_ref[...], wbr_ref[n], preferred_element_type=F32)
        gate = _sigmoid(gt_ref[:, n * D_MODEL:(n + 1) * D_MODEL].astype(F32) + bg_ref[n:n + 1, :])
        mix = gate * proj if mix is None else mix + gate * proj
    mo = jnp.dot(mix.astype(BF16), wout_ref[...], preferred_element_type=F32)
    x1 = _layer_norm_rows(DN_ALPHA * x_ref[...] + mo, lng_ref[...], lnb_ref[...])
    x1_ref[...] = x1

    xh = x1.astype(BF16)
    xl = (x1 - xh.astype(F32)).astype(BF16)
    pa = jnp.dot(xh, wr_ref[...], preferred_element_type=F32)
    pb = jnp.dot(xl, wr_ref[:, 0:LANES], preferred_element_type=F32)
    logits = pa[:, 0:LANES] + pa[:, LANES:2 * LANES] + pb + br_ref[...]
    lane = lax.broadcasted_iota(jnp.int32, logits.shape, 1)
    big = jnp.int32(1 << 20)
    is_g = lane < N_GROUPS
    gl = jnp.where(is_g, logits, NEG_BIG)
    gmax = jnp.max(gl, axis=1, keepdims=True)
    g_idx = jnp.min(jnp.where(is_g & (gl == gmax), lane, big), axis=1, keepdims=True)
    p_group = 1.0 / jnp.sum(jnp.where(is_g, jnp.exp(gl - gmax), 0.0), axis=1, keepdims=True)
    lo = N_GROUPS + EXPERTS_PER_GROUP * g_idx
    sel = (lane >= lo) & (lane < lo + EXPERTS_PER_GROUP)
    el = jnp.where(sel, logits, NEG_BIG)
    v1 = jnp.max(el, axis=1, keepdims=True)
    i1 = jnp.min(jnp.where(sel & (el == v1), lane, big), axis=1, keepdims=True)
    sel2 = sel & (lane != i1)
    el2 = jnp.where(sel2, logits, NEG_BIG)
    v2 = jnp.max(el2, axis=1, keepdims=True)
    i2 = jnp.min(jnp.where(sel2 & (el2 == v2), lane, big), axis=1, keepdims=True)
    e2 = jnp.exp(v2 - v1)
    w1 = p_group / (1.0 + e2)
    w2 = p_group * e2 / (1.0 + e2)
    eid_ref[...] = jnp.where(lane == 0, i1 - N_GROUPS, jnp.where(lane == 1, i2 - N_GROUPS, 0))
    wts_ref[...] = jnp.where(lane == 0, w1, jnp.where(lane == 1, w2, 0.0))


def _merge(x2, ya, yb, yc, gates, w_branch, b_gate, w_out, ln_g, ln_b, w_route, b_route):
    T = x2.shape[0]
    tm = min(T, 512)
    row = lambda i: (i, 0)
    const2 = lambda i: (0, 0)
    return pl.pallas_call(
        _merge_kernel,
        grid=(T // tm,),
        in_specs=[pl.BlockSpec((tm, D_MODEL), row),
                  pl.BlockSpec((tm, MIX_WIDTH), row),
                  pl.BlockSpec((tm, MIX_WIDTH), row),
                  pl.BlockSpec((tm, MIX_WIDTH), row),
                  pl.BlockSpec((tm, N_BRANCH * D_MODEL), row),
                  pl.BlockSpec((N_BRANCH, MIX_WIDTH, D_MODEL), lambda i: (0, 0, 0)),
                  pl.BlockSpec((N_BRANCH, D_MODEL), const2),
                  pl.BlockSpec((D_MODEL, D_MODEL), const2),
                  pl.BlockSpec((1, D_MODEL), const2),
                  pl.BlockSpec((1, D_MODEL), const2),
                  pl.BlockSpec((D_MODEL, 2 * LANES), const2),
                  pl.BlockSpec((1, LANES), const2)],
        out_specs=[pl.BlockSpec((tm, D_MODEL), row), pl.BlockSpec((tm, LANES), row),
                   pl.BlockSpec((tm, LANES), row)],
        out_shape=[jax.ShapeDtypeStruct((T, D_MODEL), F32), jax.ShapeDtypeStruct((T, LANES), jnp.int32),
                   jax.ShapeDtypeStruct((T, LANES), F32)],
        compiler_params=_cparams(("parallel",)),
        name="merge_router",
    )(x2, ya, yb, yc, gates, w_branch, b_gate, w_out, ln_g.reshape(1, -1), ln_b.reshape(1, -1),
      w_route, b_route)


_TOK_TILE = 256
_FFN_TILE = 512


def _rank_kernel(eid_ref, rt_ref, cnt_ref, base_ref):
    tm = eid_ref.shape[0]

    @pl.when(pl.program_id(0) == 0)
    def _():
        base_ref[...] = jnp.zeros_like(base_ref)

    lane = lax.broadcasted_iota(jnp.int32, (tm, LANES), 1)
    e1 = eid_ref[:, 0:1]
    e2 = eid_ref[:, 1:2]
    oh1 = lane == e1
    oh2 = lane == e2
    oh = jnp.where(oh1 | oh2, 1.0, 0.0)
    rr = lax.broadcasted_iota(jnp.int32, (tm, tm), 0)
    cc = lax.broadcasted_iota(jnp.int32, (tm, tm), 1)
    before = jnp.where(rr > cc, 1.0, 0.0).astype(BF16)
    pref = jnp.dot(before, oh.astype(BF16), preferred_element_type=F32) + base_ref[...]
    r1 = jnp.sum(jnp.where(oh1, pref, 0.0), axis=1, keepdims=True)
    r2 = jnp.sum(jnp.where(oh2, pref, 0.0), axis=1, keepdims=True)
    slab = jnp.where(lane == 0, e1.astype(F32),
                     jnp.where(lane == 1, e2.astype(F32),
                               jnp.where(lane == 2, r1, jnp.where(lane == 3, r2, 0.0))))
    rt_ref[0] = slab.T[0:SUBLANES, :].astype(jnp.int32)
    base_ref[...] += jnp.sum(oh, axis=0, keepdims=True)
    cnt_ref[...] = base_ref[...]


def _route_ranks(eid):
    T = eid.shape[0]
    tm = min(T, _TOK_TILE)
    nt = T // tm
    return pl.pallas_call(
        _rank_kernel,
        grid=(nt,),
        in_specs=[pl.BlockSpec((tm, LANES), lambda i: (i, 0))],
        out_specs=[pl.BlockSpec((1, SUBLANES, tm), lambda i: (i, 0, 0)),
                   pl.BlockSpec((1, LANES), lambda i: (0, 0))],
        out_shape=[jax.ShapeDtypeStruct((nt, SUBLANES, tm), jnp.int32),
                   jax.ShapeDtypeStruct((1, LANES), F32)],
        scratch_shapes=[pltpu.VMEM((1, LANES), F32)],
        compiler_params=_cparams(("arbitrary",)),
        name="route_ranks",
    )(eid)


def _dest_kernel(off_ref, rt_ref, d_ref):
    e = rt_ref[:, 0:2, :]
    dest = rt_ref[:, 2:4, :]
    for k in range(N_EXPERTS):
        dest = dest + jnp.where(e == k, off_ref[k], 0)
    d_ref[...] = dest


def _route_dests(offsets, rt):
    nt, _, tm = rt.shape
    return pl.pallas_call(
        _dest_kernel,
        grid_spec=pltpu.PrefetchScalarGridSpec(
            num_scalar_prefetch=1, grid=(1,),
            in_specs=[pl.BlockSpec((nt, SUBLANES, tm), lambda i, off: (0, 0, 0))],
            out_specs=pl.BlockSpec((nt, 2, tm), lambda i, off: (0, 0, 0))),
        out_shape=jax.ShapeDtypeStruct((nt, 2, tm), jnp.int32),
        compiler_params=_cparams(("arbitrary",)),
        name="route_dests",
    )(offsets, rt)


def _row_copy(src, dst, sem):
    return pltpu.make_async_copy(src, dst, sem)


def _scatter_kernel(tails_ref, dest_ref, x1_ref, xs_ref, xbuf_ref, zero_ref, sems):
    i = pl.program_id(0)
    nt = pl.num_programs(0)
    tm = x1_ref.shape[0]
    slot = i % 2

    def zero_copy(e):
        start = pl.multiple_of(tails_ref[e] // SUBLANES * SUBLANES, SUBLANES)
        return _row_copy(zero_ref, xs_ref.at[pl.ds(start, _FFN_TILE)], sems.at[2])

    @pl.when(i == 0)
    def _():
        zero_ref[...] = jnp.zeros_like(zero_ref)
        for e in range(N_EXPERTS):
            zero_copy(e).start()
        for e in range(N_EXPERTS):
            zero_copy(e).wait()

    xbuf_ref[slot] = x1_ref[...]

    def issue(t, c):
        for k in range(2):
            d = dest_ref[0, k, t]
            _row_copy(xbuf_ref.at[slot, pl.ds(t, 1)], xs_ref.at[pl.ds(d, 1)], sems.at[slot]).start()
        return c

    lax.fori_loop(0, tm, issue, 0, unroll=8)

    def drain(s):
        for _ in range(2):
            _row_copy(xbuf_ref.at[s], xs_ref.at[pl.ds(0, tm)], sems.at[s]).wait()

    @pl.when(i > 0)
    def _():
        drain(1 - slot)

    @pl.when(i == nt - 1)
    def _():
        drain(slot)


def _scatter_rows(tails, dest, x1, n_rows):
    nt, _, tm = dest.shape
    return pl.pallas_call(
        _scatter_kernel,
        grid_spec=pltpu.PrefetchScalarGridSpec(
            num_scalar_prefetch=1, grid=(nt,),
            in_specs=[pl.BlockSpec((1, 2, tm), lambda i, tl: (i, 0, 0), memory_space=pltpu.SMEM),
                      pl.BlockSpec((tm, D_MODEL), lambda i, tl: (i, 0))],
            out_specs=pl.BlockSpec(memory_space=pl.ANY),
            scratch_shapes=[pltpu.VMEM((2, tm, D_MODEL), F32),
                            pltpu.VMEM((_FFN_TILE, D_MODEL), F32),
                            pltpu.SemaphoreType.DMA((3,))]),
        out_shape=jax.ShapeDtypeStruct((n_rows, D_MODEL), F32),
        compiler_params=_cparams(("arbitrary",)),
        name="moe_scatter",
    )(tails, dest, x1)


def _ffn_kernel(te_ref, nu_ref, xs_ref, w1_ref, w3_ref, w2_ref, ys_ref):
    @pl.when(pl.program_id(0) < nu_ref[0])
    def _():
        xb = xs_ref[...].astype(BF16)
        h1 = jnp.dot(xb, w1_ref[0], preferred_element_type=F32)
        h3 = jnp.dot(xb, w3_ref[0], preferred_element_type=F32)
        hh = (h1 * _sigmoid(h1) * h3).astype(BF16)
        ys_ref[...] = jnp.dot(hh, w2_ref[0], preferred_element_type=F32)


def _expert_ffn(tile_expert, n_used, xs, w1, w3, w2):
    n_rows = xs.shape[0]
    n_tiles = n_rows // _FFN_TILE
    rows = lambda i, te, nu: (jnp.minimum(i, nu[0] - 1), 0)
    wsel = lambda i, te, nu: (te[jnp.minimum(i, n_tiles - 1)], 0, 0)
    return pl.pallas_call(
        _ffn_kernel,
        grid_spec=pltpu.PrefetchScalarGridSpec(
            num_scalar_prefetch=2, grid=(n_tiles,),
            in_specs=[pl.BlockSpec((_FFN_TILE, D_MODEL), rows),
                      pl.BlockSpec((1, D_MODEL, D_EXPERT), wsel),
                      pl.BlockSpec((1, D_MODEL, D_EXPERT), wsel),
                      pl.BlockSpec((1, D_EXPERT, D_MODEL), wsel)],
            out_specs=pl.BlockSpec((_FFN_TILE, D_MODEL), rows)),
        out_shape=jax.ShapeDtypeStruct((n_rows, D_MODEL), F32),
        compiler_params=_cparams(("arbitrary",)),
        name="moe_expert_ffn",
    )(tile_expert, n_used, xs, w1, w3, w2)


def _combine_kernel(dcur_ref, dnext_ref, x1_ref, wts_ref, ys_ref, lng_ref, lnb_ref, o_ref, ybuf_ref, sems):
    i = pl.program_id(0)
    nt = pl.num_programs(0)
    tm = x1_ref.shape[0]
    slot = i % 2

    def issue(d_ref, s):
        def body(t, c):
            for k in range(2):
                d = d_ref[0, k, t]
                _row_copy(ys_ref.at[pl.ds(d, 1)], ybuf_ref.at[s, pl.ds(k * tm + t, 1)], sems.at[s]).start()
            return c
        lax.fori_loop(0, tm, body, 0, unroll=8)

    @pl.when(i == 0)
    def _():
        issue(dcur_ref, 0)

    @pl.when(i + 1 < nt)
    def _():
        issue(dnext_ref, 1 - slot)

    _row_copy(ys_ref.at[pl.ds(0, 2 * tm)], ybuf_ref.at[slot], sems.at[slot]).wait()

    w1 = wts_ref[:, 0:1]
    w2 = wts_ref[:, 1:2]
    moe = w1 * ybuf_ref[slot, 0:tm, :] + w2 * ybuf_ref[slot, tm:2 * tm, :]
    o_ref[...] = _layer_norm_rows(DN_ALPHA * x1_ref[...] + moe, lng_ref[...], lnb_ref[...])


def _combine(dest, x1, wts, ys, ln_g, ln_b):
    T = x1.shape[0]
    nt, _, tm = dest.shape
    dspec = lambda f: pl.BlockSpec((1, 2, tm), f, memory_space=pltpu.SMEM)
    return pl.pallas_call(
        _combine_kernel,
        grid=(nt,),
        in_specs=[dspec(lambda i: (i, 0, 0)),
                  dspec(lambda i: (jnp.minimum(i + 1, nt - 1), 0, 0)),
                  pl.BlockSpec((tm, D_MODEL), lambda i: (i, 0)),
                  pl.BlockSpec((tm, LANES), lambda i: (i, 0)),
                  pl.BlockSpec(memory_space=pl.ANY),
                  pl.BlockSpec((1, D_MODEL), lambda i: (0, 0)),
                  pl.BlockSpec((1, D_MODEL), lambda i: (0, 0))],
        out_specs=pl.BlockSpec((tm, D_MODEL), lambda i: (i, 0)),
        out_shape=jax.ShapeDtypeStruct((T, D_MODEL), F32),
        scratch_shapes=[pltpu.VMEM((2, 2 * tm, D_MODEL), F32), pltpu.SemaphoreType.DMA((2,))],
        compiler_params=_cparams(("arbitrary",)),
        name="moe_combine",
    )(dest, dest, x1, wts, ys, ln_g.reshape(1, -1), ln_b.reshape(1, -1))


def _moe(x1, eid, wts, w1, w3, w2, ln_g, ln_b):
    T = x1.shape[0]
    rt, counts = _route_ranks(eid)
    cnt = counts[0, :N_EXPERTS].astype(jnp.int32)
    padded = (cnt + _FFN_TILE - 1) // _FFN_TILE * _FFN_TILE
    ends = jnp.cumsum(padded)
    offsets = ends - padded
    n_rows = 2 * T + (N_EXPERTS + 1) * _FFN_TILE
    n_tiles = n_rows // _FFN_TILE
    tile_start = jnp.arange(n_tiles, dtype=jnp.int32) * _FFN_TILE
    tile_expert = jnp.minimum(jnp.sum(tile_start[:, None] >= ends[None, :], axis=1), N_EXPERTS - 1)
    n_used = (ends[-1:] // _FFN_TILE).astype(jnp.int32)
    dest = _route_dests(offsets.astype(jnp.int32), rt)
    xs = _scatter_rows((offsets + cnt).astype(jnp.int32), dest, x1, n_rows)
    ys = _expert_ffn(tile_expert.astype(jnp.int32), n_used, xs, w1, w3, w2)
    return _combine(dest, x1, wts, ys, ln_g, ln_b)


def _router_weights(router_group, router_group_b, router_expert, router_expert_b):
    we = jnp.transpose(router_expert, (1, 0, 2)).reshape(D_MODEL, N_EXPERTS)
    w = jnp.concatenate([router_group, we], axis=1)
    b = jnp.concatenate([router_group_b, router_expert_b.reshape(N_EXPERTS)])
    pad = LANES - _ROUTE_COLS
    w = jnp.pad(w, ((0, 0), (0, pad)))
    w_hi = w.astype(BF16)
    w_lo = (w - w_hi.astype(F32)).astype(BF16)
    return jnp.concatenate([w_hi, w_lo], axis=1), jnp.pad(b, (0, pad)).reshape(1, LANES)


def kernel(x, positions, w_in, conv_w, conv_b, conv_ln_g, conv_ln_b, da_lambda, da_norm_g, hg_lb, hg_norm_g, w_branch, b_gate, w_out, ln1_g, ln1_b, router_group, router_group_b, router_expert, router_expert_b, exp_w1, exp_w3, exp_w2, ln2_g, ln2_b):
    B, S, D = x.shape
    T = B * S
    rope = _rope_tables(positions)
    lb_tab = jnp.cumsum(jax.nn.softmax(hg_lb.astype(F32), axis=0), axis=0)
    lb_tab = lb_tab - lb_tab[0:1]
    x2 = x.reshape(T, D)
    for l in range(DEPTH):
        a_in, q, k, v, hq, hf, hi, hg, gates = _in_projection(x2, w_in[l].astype(BF16), rope)
        y_a = _conformer_conv(a_in, conv_w[l], conv_b[l], conv_ln_g[l], conv_ln_b[l], B, S)
        y_b = _diff_attention(q, k, v, da_lambda[l], da_norm_g[l], l, B, S)
        y_c = _hgrn2(hq, hf, hi, hg, lb_tab[l], hg_norm_g[l], B, S)
        w_route, b_route = _router_weights(router_group[l], router_group_b[l],
                                           router_expert[l], router_expert_b[l])
        x1, eid, wts = _merge(x2, y_a, y_b, y_c, gates, w_branch[l].astype(BF16), b_gate[l],
                          w_out[l].astype(BF16), ln1_g[l], ln1_b[l], w_route, b_route)
        x2 = _moe(x1, eid, wts, exp_w1[l].astype(BF16), exp_w3[l].astype(BF16), exp_w2[l].astype(BF16),
                  ln2_g[l], ln2_b[l])
    return x2.reshape(B, S, D)
```

```python
import functools
import math

import jax
import jax.numpy as jnp
from jax import lax
from jax.experimental import pallas as pl
from jax.experimental.pallas import tpu as pltpu

F32 = jnp.float32
BF16 = jnp.bfloat16

D_MODEL = 1024
DEPTH = 2
CONV_CH = 512
CONV_K = 31
DA_HEADS = 4
DA_SUB = 64
ROT_DIM = DA_SUB // 4
ROPE_THETA = 500000.0
HG_HEADS = 4
HG_DK = 128
MIX_WIDTH = 512
N_BRANCH = 3
N_GROUPS = 4
EXPERTS_PER_GROUP = 8
N_EXPERTS = N_GROUPS * EXPERTS_PER_GROUP
D_EXPERT = 512
DN_ALPHA = (2 * DEPTH) ** 0.25
LN_EPS = 1e-5
IN_COLS = 2 * CONV_CH + 7 * MIX_WIDTH + N_BRANCH * D_MODEL

LANES = 128
SUBLANES = 8
VMEM_LIMIT_BYTES = 56 * 1024 * 1024

NEG_BIG = -1e30


def _cparams(sem):
    return pltpu.CompilerParams(dimension_semantics=sem, vmem_limit_bytes=VMEM_LIMIT_BYTES)


def _layer_norm_rows(r, g, b):
    mu = jnp.mean(r, axis=-1, keepdims=True)
    d = r - mu
    var = jnp.mean(d * d, axis=-1, keepdims=True)
    return d * lax.rsqrt(var + LN_EPS) * g + b


def _sigmoid(x):
    return 1.0 / (1.0 + jnp.exp(-x))


def _rope_kernel(pos_ref, freq_ref, c_ref, sa_ref, sb_ref):
    ang = pos_ref[...].astype(F32) * freq_ref[...]
    lane = lax.broadcasted_iota(jnp.int32, ang.shape, 1) & (DA_SUB - 1)
    half = ROT_DIM // 2
    cs = jnp.cos(ang)
    sn = jnp.sin(ang)
    c_ref[...] = jnp.where(lane < ROT_DIM, cs, 1.0)
    sa_ref[...] = jnp.where(lane < half, -sn, 0.0)
    sb_ref[...] = jnp.where((lane >= half) & (lane < ROT_DIM), sn, 0.0)


def _rope_tables(positions):
    T = positions.size
    tm = min(T, 1024)
    inv_freq = ROPE_THETA ** (-jnp.arange(0, ROT_DIM, 2, dtype=F32) / ROT_DIM)
    freq_lane = jnp.tile(inv_freq, LANES // (ROT_DIM // 2))[None, :]
    pos = positions.reshape(T, 1)
    out = jax.ShapeDtypeStruct((T, LANES), F32)
    return pl.pallas_call(
        _rope_kernel,
        grid=(T // tm,),
        in_specs=[pl.BlockSpec((tm, 1), lambda i: (i, 0)),
                  pl.BlockSpec((1, LANES), lambda i: (0, 0))],
        out_specs=[pl.BlockSpec((tm, LANES), lambda i: (i, 0))] * 3,
        out_shape=[out, out, out],
        compiler_params=_cparams(("parallel",)),
        name="rope_tables",
    )(pos, freq_lane)


_COL_A = 0
_COL_Q = 2 * CONV_CH
_COL_K = _COL_Q + MIX_WIDTH
_COL_V = _COL_K + MIX_WIDTH
_COL_HQ = _COL_V + MIX_WIDTH
_COL_HF = _COL_HQ + MIX_WIDTH
_COL_HI = _COL_HF + MIX_WIDTH
_COL_HG = _COL_HI + MIX_WIDTH
_COL_GATE = _COL_HG + MIX_WIDTH


def _inproj_kernel(x_ref, w_ref, c_ref, sa_ref, sb_ref,
                   a_ref, q_ref, k_ref, v_ref, hq_ref, hf_ref, hi_ref, hg_ref, gt_ref):
    xb = x_ref[...].astype(BF16)

    def mm(c0, width):
        return jnp.dot(xb, w_ref[:, c0:c0 + width], preferred_element_type=F32)

    def rotary(t, scale):
        c, sa, sb = c_ref[...], sa_ref[...], sb_ref[...]
        outs = []
        for h in range(DA_HEADS):
            th = t[:, h * LANES:(h + 1) * LANES]
            r = th * c + pltpu.roll(th, LANES - ROT_DIM // 2, 1) * sa + pltpu.roll(th, ROT_DIM // 2, 1) * sb
            outs.append(r * scale)
        return jnp.concatenate(outs, axis=1)

    for j in range(2):
        a_ref[:, j * CONV_CH:(j + 1) * CONV_CH] = mm(_COL_A + j * CONV_CH, CONV_CH).astype(BF16)
    q_ref[...] = rotary(mm(_COL_Q, MIX_WIDTH), DA_SUB ** -0.5 * math.log2(math.e)).astype(BF16)
    k_ref[...] = rotary(mm(_COL_K, MIX_WIDTH), 1.0).astype(BF16)
    v_ref[...] = mm(_COL_V, MIX_WIDTH).astype(BF16)
    hq_ref[...] = mm(_COL_HQ, MIX_WIDTH).astype(BF16)
    hf_ref[...] = mm(_COL_HF, MIX_WIDTH).astype(BF16)
    hi_ref[...] = mm(_COL_HI, MIX_WIDTH).astype(BF16)
    hg_ref[...] = mm(_COL_HG, MIX_WIDTH).astype(BF16)
    for j in range(N_BRANCH * D_MODEL // MIX_WIDTH):
        gt_ref[:, j * MIX_WIDTH:(j + 1) * MIX_WIDTH] = mm(_COL_GATE + j * MIX_WIDTH, MIX_WIDTH).astype(BF16)


def _in_projection(x2, w_in_bf16, rope):
    T = x2.shape[0]
    tm = min(T, 512)
    widths = [2 * CONV_CH] + [MIX_WIDTH] * 7 + [N_BRANCH * D_MODEL]
    row = lambda i: (i, 0)
    return pl.pallas_call(
        _inproj_kernel,
        grid=(T // tm,),
        in_specs=[pl.BlockSpec((tm, D_MODEL), row),
                  pl.BlockSpec((D_MODEL, IN_COLS), lambda i: (0, 0), pipeline_mode=pl.Buffered(1)),
                  pl.BlockSpec((tm, LANES), row),
                  pl.BlockSpec((tm, LANES), row),
                  pl.BlockSpec((tm, LANES), row)],
        out_specs=[pl.BlockSpec((tm, w), row) for w in widths],
        out_shape=[jax.ShapeDtypeStruct((T, w), BF16) for w in widths],
        compiler_params=_cparams(("parallel",)),
        name="in_projection",
    )(x2, w_in_bf16, *rope)


_HALO = 32
_CONV_ROWS = 32


def _conv_kernel(a_ref, halo_ref, w_ref, cb_ref, g_ref, b_ref, o_ref, ush_ref, y_ref, *, ts):
    j = pl.program_id(1)

    def glu(a):
        a = a.astype(F32)
        return a[:, :CONV_CH] * _sigmoid(a[:, CONV_CH:])

    u_cur = glu(a_ref[...])
    u_halo = jnp.where(j > 0, glu(halo_ref[...]), 0.0)
    u_ext = jnp.concatenate([u_halo, u_cur], axis=0)
    n_al = ts + _HALO - SUBLANES
    ush_ref[0] = u_ext
    for s in range(1, SUBLANES):
        ush_ref[s, 0:n_al, :] = u_ext[s:s + n_al, :]

    off0 = _HALO - (CONV_K - 1)

    def rows(r, carry):
        r0 = pl.multiple_of(r * _CONV_ROWS, _CONV_ROWS)
        for cb in range(CONV_CH // LANES):
            cs = slice(cb * LANES, (cb + 1) * LANES)
            acc = jnp.zeros((_CONV_ROWS, LANES), F32)
            for k in range(CONV_K):
                off = off0 + k
                s, al = off % SUBLANES, (off // SUBLANES) * SUBLANES
                acc = acc + w_ref[k:k + 1, cs] * ush_ref[s, pl.ds(r0 + al, _CONV_ROWS), cs]
            y_ref[pl.ds(r0, _CONV_ROWS), cs] = acc
        return carry

    lax.fori_loop(0, ts // _CONV_ROWS, rows, 0)
    y = y_ref[...] + cb_ref[...]
    yn = _layer_norm_rows(y, g_ref[...], b_ref[...])
    o_ref[...] = (yn * _sigmoid(yn)).astype(BF16)


def _conformer_conv(a_in, conv_w, conv_b, ln_g, ln_b, B, S):
    T = B * S
    ts = min(S, 512)
    nt = S // ts
    vec = lambda v: v.reshape(1, CONV_CH)
    return pl.pallas_call(
        functools.partial(_conv_kernel, ts=ts),
        grid=(B, nt),
        in_specs=[pl.BlockSpec((ts, 2 * CONV_CH), lambda b, j: (b * nt + j, 0)),
                  pl.BlockSpec((_HALO, 2 * CONV_CH),
                               lambda b, j: (jnp.maximum((b * nt + j) * (ts // _HALO) - 1, 0), 0)),
                  pl.BlockSpec((CONV_K, CONV_CH), lambda b, j: (0, 0)),
                  pl.BlockSpec((1, CONV_CH), lambda b, j: (0, 0)),
                  pl.BlockSpec((1, CONV_CH), lambda b, j: (0, 0)),
                  pl.BlockSpec((1, CONV_CH), lambda b, j: (0, 0))],
        out_specs=pl.BlockSpec((ts, CONV_CH), lambda b, j: (b * nt + j, 0)),
        out_shape=jax.ShapeDtypeStruct((T, CONV_CH), BF16),
        scratch_shapes=[pltpu.VMEM((SUBLANES, ts + _HALO, CONV_CH), F32),
                        pltpu.VMEM((ts, CONV_CH), F32)],
        compiler_params=_cparams(("parallel", "parallel")),
        name="conformer_conv",
    )(a_in, a_in, conv_w, vec(conv_b), vec(ln_g), vec(ln_b))


def _attn_kernel(q_ref, k_ref, v_ref, lam_ref, g_ref, o_ref, vt_ref, *, tq, tk, lam_init):
    qi = pl.program_id(2)
    n_diag = tq // tk

    @pl.when(qi == 0)
    def _():
        vt_ref[...] = v_ref[...].astype(F32).T.astype(BF16)

    q = q_ref[...]
    lane = lax.broadcasted_iota(jnp.int32, q.shape, 1)
    zero = jnp.zeros_like(q)
    qq = jnp.concatenate([jnp.where(lane < DA_SUB, q, zero), jnp.where(lane >= DA_SUB, q, zero)], axis=0)

    def scores(j):
        k0 = pl.multiple_of(j * tk, tk)
        return lax.dot_general(k_ref[pl.ds(k0, tk), :], qq, (((1,), (1,)), ((), ())),
                               preferred_element_type=F32)

    def update(j, st, carry):
        m, l, acc = carry
        k0 = pl.multiple_of(j * tk, tk)
        m_new = jnp.maximum(m, jnp.max(st, axis=0, keepdims=True))
        alpha = jnp.exp2(m - m_new)
        p = jnp.exp2(st - m_new)
        l = alpha * l + jnp.sum(p, axis=0, keepdims=True)
        acc = alpha * acc + jnp.dot(vt_ref[:, pl.ds(k0, tk)], p.astype(BF16), preferred_element_type=F32)
        return m_new, l, acc

    def body(j, c):
        st_next = scores(j + 1)
        m, l, acc = update(j, c[3], c[0:3])
        return m, l, acc, st_next

    n_full = qi * n_diag
    init = (jnp.full((1, 2 * tq), NEG_BIG, F32), jnp.zeros((1, 2 * tq), F32),
            jnp.zeros((LANES, 2 * tq), F32), scores(0))
    m, l, acc, st = lax.fori_loop(0, n_full, body, init)
    q_in = lax.broadcasted_iota(jnp.int32, (1, tq), 1)
    qpos = jnp.concatenate([q_in, q_in], axis=1)
    for d in range(n_diag):
        st_next = scores(n_full + d + 1) if d + 1 < n_diag else None
        kpos = d * tk + lax.broadcasted_iota(jnp.int32, (tk, 1), 0)
        m, l, acc = update(n_full + d, jnp.where(kpos <= qpos, st, NEG_BIG), (m, l, acc))
        st = st_next
    o12 = acc / l
    lp = lam_ref[...]
    lam = (jnp.exp(jnp.sum(lp[0:1] * lp[1:2], axis=1, keepdims=True))
           - jnp.exp(jnp.sum(lp[2:3] * lp[3:4], axis=1, keepdims=True)) + lam_init)
    o = o12[:, :tq] - lam * o12[:, tq:]
    ms = jnp.mean(o * o, axis=0, keepdims=True)
    o = o * lax.rsqrt(ms + LN_EPS) * g_ref[...] * (1.0 - lam_init)
    o_ref[...] = o.T.astype(BF16)


def _diff_attention(q, k, v, lam_params, norm_g, layer_idx, B, S):
    T = B * S
    tq = min(S, 512)
    tk = min(S, 256)
    nq = S // tq
    lam_init = 0.8 - 0.6 * math.exp(-0.3 * layer_idx)
    return pl.pallas_call(
        functools.partial(_attn_kernel, tq=tq, tk=tk, lam_init=lam_init),
        grid=(B, DA_HEADS, nq),
        in_specs=[pl.BlockSpec((tq, LANES), lambda b, h, i: (b * nq + i, h)),
                  pl.BlockSpec((S, LANES), lambda b, h, i: (b, h)),
                  pl.BlockSpec((S, LANES), lambda b, h, i: (b, h)),
                  pl.BlockSpec((4, DA_SUB), lambda b, h, i: (0, 0)),
                  pl.BlockSpec((LANES, 1), lambda b, h, i: (0, 0))],
        out_specs=pl.BlockSpec((tq, LANES), lambda b, h, i: (b * nq + i, h)),
        out_shape=jax.ShapeDtypeStruct((T, MIX_WIDTH), BF16),
        scratch_shapes=[pltpu.VMEM((LANES, S), BF16)],
        compiler_params=_cparams(("parallel", "parallel", "arbitrary")),
        name="diff_attention",
    )(q, k, v, lam_params, norm_g.reshape(LANES, 1))


_HG_CHUNK = 64


_HG_SAFE_DECAY = 160.0


def _hgrn_kernel(q_ref, f_ref, i_ref, g_ref, lb_ref, ng_ref, o_ref, st_ref, b_ref, kf_ref, *, n_chunks):
    C = _HG_CHUNK

    @pl.when(pl.program_id(1) == 0)
    def _():
        st_ref[...] = jnp.zeros_like(st_ref)

    rr = lax.broadcasted_iota(jnp.int32, (C, C), 0)
    cc = lax.broadcasted_iota(jnp.int32, (C, C), 1)
    causal = rr >= cc
    tri = jnp.where(causal, 1.0, 0.0).astype(BF16)
    tiles = [(slice(ci * C, (ci + 1) * C), h, slice(h * HG_DK, (h + 1) * HG_DK))
             for ci in range(n_chunks) for h in range(HG_HEADS)]

    decay = jnp.zeros((1, HG_DK), F32)
    for rows, h, cols in tiles:
        z = f_ref[rows, cols].astype(F32)
        lb = lb_ref[:, cols]
        e = jnp.exp(-jnp.abs(z))
        log_sig = jnp.minimum(z, 0.0) - jnp.log(1.0 + e)
        ya = jnp.log(lb)
        yb = jnp.log(1.0 - lb) + log_sig
        log_f = jnp.maximum(ya, yb) + jnp.log(1.0 + jnp.exp(-jnp.abs(ya - yb)))
        kf_ref[rows, cols] = (1.0 - lb) * jnp.where(z >= 0.0, e, 1.0) / (1.0 + e)
        hi = log_f.astype(BF16)
        r1 = log_f - hi.astype(F32)
        mid = r1.astype(BF16)
        lo = (r1 - mid.astype(F32)).astype(BF16)
        b3 = jnp.dot(tri, jnp.concatenate([hi, mid, lo], axis=1), preferred_element_type=F32)
        bcum = b3[:, 0:HG_DK] + b3[:, HG_DK:2 * HG_DK] + b3[:, 2 * HG_DK:3 * HG_DK]
        b_ref[rows, cols] = bcum
        decay = jnp.maximum(decay, -bcum[C - 1:C, :])
    unsafe = jnp.max(decay) > _HG_SAFE_DECAY

    def pairwise_att(qf, bcum, r0, cols):
        def body(g, att):
            s0 = pl.multiple_of(g * SUBLANES, SUBLANES)
            kblk = kf_ref[pl.ds(r0 + s0, SUBLANES), cols]
            bblk = b_ref[pl.ds(r0 + s0, SUBLANES), cols]
            for j in range(SUBLANES):
                w = qf * kblk[j:j + 1, :] * jnp.exp(jnp.minimum(bcum - bblk[j:j + 1, :], 0.0))
                att = att + jnp.where(cc == s0 + j, jnp.sum(w, axis=1, keepdims=True), 0.0)
            return att
        return lax.fori_loop(0, C // SUBLANES, body, jnp.zeros((C, C), F32))

    def phase2(exact):
        for rows, h, cols in tiles:
            bcum = b_ref[rows, cols]
            kf = kf_ref[rows, cols]
            b_last = bcum[C - 1:C, :]
            qv = q_ref[rows, cols].astype(F32)
            qf = qv * _sigmoid(qv)
            qe = qf * jnp.exp(bcum)
            kdec = (kf * jnp.exp(b_last - bcum)).astype(BF16)
            vv = i_ref[rows, cols]
            if exact:
                att = pairwise_att(qf, bcum, rows.start, cols)
            else:
                mref = 0.5 * b_last
                qd = (qe * jnp.exp(-mref)).astype(BF16)
                kd = (kf * jnp.exp(mref - bcum)).astype(BF16)
                att = lax.dot_general(qd, kd, (((1,), (1,)), ((), ())), preferred_element_type=F32)
            att = jnp.where(causal, att, 0.0).astype(BF16)
            st = st_ref[h]
            o = (lax.dot_general(qe.astype(BF16), st.astype(BF16), (((1,), (1,)), ((), ())),
                                 preferred_element_type=F32)
                 + jnp.dot(att, vv, preferred_element_type=F32))
            st_ref[h] = st * jnp.exp(b_last) + lax.dot_general(
                vv, kdec, (((0,), (0,)), ((), ())), preferred_element_type=F32)
            ms = jnp.mean(o * o, axis=1, keepdims=True)
            gv = g_ref[rows, cols].astype(F32)
            o_ref[rows, cols] = (o * lax.rsqrt(ms + LN_EPS) * ng_ref[...] * (gv * _sigmoid(gv))).astype(BF16)

    @pl.when(unsafe)
    def _():
        phase2(True)

    @pl.when(jnp.logical_not(unsafe))
    def _():
        phase2(False)


def _hgrn2(hq, hf, hi, hg, lb, norm_g, B, S):
    T = B * S
    ts = min(S, 256)
    nt = S // ts
    blk = pl.BlockSpec((ts, MIX_WIDTH), lambda b, j: (b * nt + j, 0))
    return pl.pallas_call(
        functools.partial(_hgrn_kernel, n_chunks=ts // _HG_CHUNK),
        grid=(B, nt),
        in_specs=[blk, blk, blk, blk,
                  pl.BlockSpec((1, MIX_WIDTH), lambda b, j: (0, 0)),
                  pl.BlockSpec((1, HG_DK), lambda b, j: (0, 0))],
        out_specs=blk,
        out_shape=jax.ShapeDtypeStruct((T, MIX_WIDTH), BF16),
        scratch_shapes=[pltpu.VMEM((HG_HEADS, HG_DK, HG_DK), F32),
                        pltpu.VMEM((ts, MIX_WIDTH), F32),
                        pltpu.VMEM((ts, MIX_WIDTH), F32)],
        compiler_params=_cparams(("parallel", "arbitrary")),
        name="hgrn2",
    )(hq, hf, hi, hg, lb.reshape(1, MIX_WIDTH), norm_g.reshape(1, HG_DK))


_ROUTE_COLS = N_GROUPS + N_EXPERTS


def _merge_kernel(x_ref, ya_ref, yb_ref, yc_ref, gt_ref, wbr_ref, bg_ref, wout_ref, lng_ref, lnb_ref,
                  wr_ref, br_ref, x1_ref, eid_ref, wts_ref):
    mix = None
    for n, y_ref in enumerate((ya_ref, yb_ref, yc_ref)):
        proj = jnp.dot(y_ref[...], wbr_ref[n], preferred_element_type=F32)
        gate = _sigmoid(gt_ref[:, n * D_MODEL:(n + 1) * D_MODEL].astype(F32) + bg_ref[n:n + 1, :])
        mix = gate * proj if mix is None else mix + gate * proj
    mo = jnp.dot(mix.astype(BF16), wout_ref[...], preferred_element_type=F32)
    x1 = _layer_norm_rows(DN_ALPHA * x_ref[...] + mo, lng_ref[...], lnb_ref[...])
    x1_ref[...] = x1

    xh = x1.astype(BF16)
    xl = (x1 - xh.astype(F32)).astype(BF16)
    pa = jnp.dot(xh, wr_ref[...], preferred_element_type=F32)
    pb = jnp.dot(xl, wr_ref[:, 0:LANES], preferred_element_type=F32)
    logits = pa[:, 0:LANES] + pa[:, LANES:2 * LANES] + pb + br_ref[...]
    lane = lax.broadcasted_iota(jnp.int32, logits.shape, 1)
    big = jnp.int32(1 << 20)
    is_g = lane < N_GROUPS
    gl = jnp.where(is_g, logits, NEG_BIG)
    gmax = jnp.max(gl, axis=1, keepdims=True)
    g_idx = jnp.min(jnp.where(is_g & (gl == gmax), lane, big), axis=1, keepdims=True)
    p_group = 1.0 / jnp.sum(jnp.where(is_g, jnp.exp(gl - gmax), 0.0), axis=1, keepdims=True)
    lo = N_GROUPS + EXPERTS_PER_GROUP * g_idx
    sel = (lane >= lo) & (lane < lo + EXPERTS_PER_GROUP)
    el = jnp.where(sel, logits, NEG_BIG)
    v1 = jnp.max(el, axis=1, keepdims=True)
    i1 = jnp.min(jnp.where(sel & (el == v1), lane, big), axis=1, keepdims=True)
    sel2 = sel & (lane != i1)
    el2 = jnp.where(sel2, logits, NEG_BIG)
    v2 = jnp.max(el2, axis=1, keepdims=True)
    i2 = jnp.min(jnp.where(sel2 & (el2 == v2), lane, big), axis=1, keepdims=True)
    e2 = jnp.exp(v2 - v1)
    w1 = p_group / (1.0 + e2)
    w2 = p_group * e2 / (1.0 + e2)
    eid_ref[...] = jnp.where(lane == 0, i1 - N_GROUPS, jnp.where(lane == 1, i2 - N_GROUPS, 0))
    wts_ref[...] = jnp.where(lane == 0, w1, jnp.where(lane == 1, w2, 0.0))


def _merge(x2, ya, yb, yc, gates, w_branch, b_gate, w_out, ln_g, ln_b, w_route, b_route):
    T = x2.shape[0]
    tm = min(T, 512)
    row = lambda i: (i, 0)
    const2 = lambda i: (0, 0)
    return pl.pallas_call(
        _merge_kernel,
        grid=(T // tm,),
        in_specs=[pl.BlockSpec((tm, D_MODEL), row),
                  pl.BlockSpec((tm, MIX_WIDTH), row),
                  pl.BlockSpec((tm, MIX_WIDTH), row),
                  pl.BlockSpec((tm, MIX_WIDTH), row),
                  pl.BlockSpec((tm, N_BRANCH * D_MODEL), row),
                  pl.BlockSpec((N_BRANCH, MIX_WIDTH, D_MODEL), lambda i: (0, 0, 0)),
                  pl.BlockSpec((N_BRANCH, D_MODEL), const2),
                  pl.BlockSpec((D_MODEL, D_MODEL), const2),
                  pl.BlockSpec((1, D_MODEL), const2),
                  pl.BlockSpec((1, D_MODEL), const2),
                  pl.BlockSpec((D_MODEL, 2 * LANES), const2),
                  pl.BlockSpec((1, LANES), const2)],
        out_specs=[pl.BlockSpec((tm, D_MODEL), row), pl.BlockSpec((tm, LANES), row),
                   pl.BlockSpec((tm, LANES), row)],
        out_shape=[jax.ShapeDtypeStruct((T, D_MODEL), F32), jax.ShapeDtypeStruct((T, LANES), jnp.int32),
                   jax.ShapeDtypeStruct((T, LANES), F32)],
        compiler_params=_cparams(("parallel",)),
        name="merge_router",
    )(x2, ya, yb, yc, gates, w_branch, b_gate, w_out, ln_g.reshape(1, -1), ln_b.reshape(1, -1),
      w_route, b_route)


_TOK_TILE = 256
_FFN_TILE = 512


def _rank_kernel(eid_ref, rt_ref, cnt_ref, base_ref):
    tm = eid_ref.shape[0]

    @pl.when(pl.program_id(0) == 0)
    def _():
        base_ref[...] = jnp.zeros_like(base_ref)

    lane = lax.broadcasted_iota(jnp.int32, (tm, LANES), 1)
    e1 = eid_ref[:, 0:1]
    e2 = eid_ref[:, 1:2]
    oh1 = lane == e1
    oh2 = lane == e2
    oh = jnp.where(oh1 | oh2, 1.0, 0.0)
    rr = lax.broadcasted_iota(jnp.int32, (tm, tm), 0)
    cc = lax.broadcasted_iota(jnp.int32, (tm, tm), 1)
    before = jnp.where(rr > cc, 1.0, 0.0).astype(BF16)
    pref = jnp.dot(before, oh.astype(BF16), preferred_element_type=F32) + base_ref[...]
    r1 = jnp.sum(jnp.where(oh1, pref, 0.0), axis=1, keepdims=True)
    r2 = jnp.sum(jnp.where(oh2, pref, 0.0), axis=1, keepdims=True)
    slab = jnp.where(lane == 0, e1.astype(F32),
                     jnp.where(lane == 1, e2.astype(F32),
                               jnp.where(lane == 2, r1, jnp.where(lane == 3, r2, 0.0))))
    rt_ref[0] = slab.T[0:SUBLANES, :].astype(jnp.int32)
    base_ref[...] += jnp.sum(oh, axis=0, keepdims=True)
    cnt_ref[...] = base_ref[...]


def _route_ranks(eid):
    T = eid.shape[0]
    tm = min(T, _TOK_TILE)
    nt = T // tm
    return pl.pallas_call(
        _rank_kernel,
        grid=(nt,),
        in_specs=[pl.BlockSpec((tm, LANES), lambda i: (i, 0))],
        out_specs=[pl.BlockSpec((1, SUBLANES, tm), lambda i: (i, 0, 0)),
                   pl.BlockSpec((1, LANES), lambda i: (0, 0))],
        out_shape=[jax.ShapeDtypeStruct((nt, SUBLANES, tm), jnp.int32),
                   jax.ShapeDtypeStruct((1, LANES), F32)],
        scratch_shapes=[pltpu.VMEM((1, LANES), F32)],
        compiler_params=_cparams(("arbitrary",)),
        name="route_ranks",
    )(eid)


def _dest_kernel(off_ref, rt_ref, d_ref):
    e = rt_ref[:, 0:2, :]
    dest = rt_ref[:, 2:4, :]
    for k in range(N_EXPERTS):
        dest = dest + jnp.where(e == k, off_ref[k], 0)
    d_ref[...] = dest


def _route_dests(offsets, rt):
    nt, _, tm = rt.shape
    return pl.pallas_call(
        _dest_kernel,
        grid_spec=pltpu.PrefetchScalarGridSpec(
            num_scalar_prefetch=1, grid=(1,),
            in_specs=[pl.BlockSpec((nt, SUBLANES, tm), lambda i, off: (0, 0, 0))],
            out_specs=pl.BlockSpec((nt, 2, tm), lambda i, off: (0, 0, 0))),
        out_shape=jax.ShapeDtypeStruct((nt, 2, tm), jnp.int32),
        compiler_params=_cparams(("arbitrary",)),
        name="route_dests",
    )(offsets, rt)


def _row_copy(src, dst, sem):
    return pltpu.make_async_copy(src, dst, sem)


def _scatter_kernel(tails_ref, dest_ref, x1_ref, xs_ref, xbuf_ref, zero_ref, sems):
    i = pl.program_id(0)
    nt = pl.num_programs(0)
    tm = x1_ref.shape[0]
    slot = i % 2

    def zero_copy(e):
        start = pl.multiple_of(tails_ref[e] // SUBLANES * SUBLANES, SUBLANES)
        return _row_copy(zero_ref, xs_ref.at[pl.ds(start, _FFN_TILE)], sems.at[2])

    @pl.when(i == 0)
    def _():
        zero_ref[...] = jnp.zeros_like(zero_ref)
        for e in range(N_EXPERTS):
            zero_copy(e).start()
        for e in range(N_EXPERTS):
            zero_copy(e).wait()

    xbuf_ref[slot] = x1_ref[...]

    def issue(g, c):
        t0 = pl.multiple_of(g * SUBLANES, SUBLANES)
        for j in range(SUBLANES):
            for k in range(2):
                d = dest_ref[0, k, t0 + j]
                _row_copy(xbuf_ref.at[slot, pl.ds(t0 + j, 1)], xs_ref.at[pl.ds(d, 1)], sems.at[slot]).start()
        return c

    lax.fori_loop(0, tm // SUBLANES, issue, 0)

    def drain(s):
        for _ in range(2):
            _row_copy(xbuf_ref.at[s], xs_ref.at[pl.ds(0, tm)], sems.at[s]).wait()

    @pl.when(i > 0)
    def _():
        drain(1 - slot)

    @pl.when(i == nt - 1)
    def _():
        drain(slot)


def _scatter_rows(tails, dest, x1, n_rows):
    nt, _, tm = dest.shape
    return pl.pallas_call(
        _scatter_kernel,
        grid_spec=pltpu.PrefetchScalarGridSpec(
            num_scalar_prefetch=1, grid=(nt,),
            in_specs=[pl.BlockSpec((1, 2, tm), lambda i, tl: (i, 0, 0), memory_space=pltpu.SMEM),
                      pl.BlockSpec((tm, D_MODEL), lambda i, tl: (i, 0))],
            out_specs=pl.BlockSpec(memory_space=pl.ANY),
            scratch_shapes=[pltpu.VMEM((2, tm, D_MODEL), F32),
                            pltpu.VMEM((_FFN_TILE, D_MODEL), F32),
                            pltpu.SemaphoreType.DMA((3,))]),
        out_shape=jax.ShapeDtypeStruct((n_rows, D_MODEL), F32),
        compiler_params=_cparams(("arbitrary",)),
        name="moe_scatter",
    )(tails, dest, x1)


def _ffn_kernel(te_ref, nu_ref, xs_ref, w1_ref, w3_ref, w2_ref, ys_ref, w13_ref, w2b_ref):
    i = pl.program_id(0)

    @pl.when((i == 0) | (te_ref[i] != te_ref[jnp.maximum(i - 1, 0)]))
    def _():
        w13_ref[0] = w1_ref[0, 0].astype(BF16)
        w13_ref[1] = w3_ref[0, 0].astype(BF16)
        w2b_ref[...] = w2_ref[0, 0].astype(BF16)

    @pl.when(i < nu_ref[0])
    def _():
        xb = xs_ref[...].astype(BF16)
        h1 = jnp.dot(xb, w13_ref[0], preferred_element_type=F32)
        h3 = jnp.dot(xb, w13_ref[1], preferred_element_type=F32)
        hh = (h1 * _sigmoid(h1) * h3).astype(BF16)
        ys_ref[...] = jnp.dot(hh, w2b_ref[...], preferred_element_type=F32)


def _expert_ffn(tile_expert, n_used, xs, w1, w3, w2, layer):
    n_rows = xs.shape[0]
    n_tiles = n_rows // _FFN_TILE
    rows = lambda i, te, nu: (jnp.minimum(i, nu[0] - 1), 0)
    wsel = lambda i, te, nu: (layer, te[jnp.minimum(i, n_tiles - 1)], 0, 0)
    return pl.pallas_call(
        _ffn_kernel,
        grid_spec=pltpu.PrefetchScalarGridSpec(
            num_scalar_prefetch=2, grid=(n_tiles,),
            in_specs=[pl.BlockSpec((_FFN_TILE, D_MODEL), rows),
                      pl.BlockSpec((1, 1, D_MODEL, D_EXPERT), wsel),
                      pl.BlockSpec((1, 1, D_MODEL, D_EXPERT), wsel),
                      pl.BlockSpec((1, 1, D_EXPERT, D_MODEL), wsel)],
            out_specs=pl.BlockSpec((_FFN_TILE, D_MODEL), rows),
            scratch_shapes=[pltpu.VMEM((2, D_MODEL, D_EXPERT), BF16),
                            pltpu.VMEM((D_EXPERT, D_MODEL), BF16)]),
        out_shape=jax.ShapeDtypeStruct((n_rows, D_MODEL), F32),
        compiler_params=_cparams(("arbitrary",)),
        name="moe_expert_ffn",
    )(tile_expert, n_used, xs, w1, w3, w2)


def _combine_kernel(dcur_ref, dnext_ref, x1_ref, wts_ref, ys_ref, lng_ref, lnb_ref, o_ref, ybuf_ref, sems):
    i = pl.program_id(0)
    nt = pl.num_programs(0)
    tm = x1_ref.shape[0]
    slot = i % 2

    def issue(d_ref, s):
        def body(g, c):
            t0 = pl.multiple_of(g * SUBLANES, SUBLANES)
            for j in range(SUBLANES):
                for k in range(2):
                    d = d_ref[0, k, t0 + j]
                    _row_copy(ys_ref.at[pl.ds(d, 1)], ybuf_ref.at[s, pl.ds(k * tm + t0 + j, 1)],
                              sems.at[s]).start()
            return c
        lax.fori_loop(0, tm // SUBLANES, body, 0)

    @pl.when(i == 0)
    def _():
        issue(dcur_ref, 0)

    @pl.when(i + 1 < nt)
    def _():
        issue(dnext_ref, 1 - slot)

    _row_copy(ys_ref.at[pl.ds(0, 2 * tm)], ybuf_ref.at[slot], sems.at[slot]).wait()

    w1 = wts_ref[:, 0:1]
    w2 = wts_ref[:, 1:2]
    moe = w1 * ybuf_ref[slot, 0:tm, :] + w2 * ybuf_ref[slot, tm:2 * tm, :]
    o_ref[...] = _layer_norm_rows(DN_ALPHA * x1_ref[...] + moe, lng_ref[...], lnb_ref[...])


def _combine(dest, x1, wts, ys, ln_g, ln_b):
    T = x1.shape[0]
    nt, _, tm = dest.shape
    dspec = lambda f: pl.BlockSpec((1, 2, tm), f, memory_space=pltpu.SMEM)
    return pl.pallas_call(
        _combine_kernel,
        grid=(nt,),
        in_specs=[dspec(lambda i: (i, 0, 0)),
                  dspec(lambda i: (jnp.minimum(i + 1, nt - 1), 0, 0)),
                  pl.BlockSpec((tm, D_MODEL), lambda i: (i, 0)),
                  pl.BlockSpec((tm, LANES), lambda i: (i, 0)),
                  pl.BlockSpec(memory_space=pl.ANY),
                  pl.BlockSpec((1, D_MODEL), lambda i: (0, 0)),
                  pl.BlockSpec((1, D_MODEL), lambda i: (0, 0))],
        out_specs=pl.BlockSpec((tm, D_MODEL), lambda i: (i, 0)),
        out_shape=jax.ShapeDtypeStruct((T, D_MODEL), F32),
        scratch_shapes=[pltpu.VMEM((2, 2 * tm, D_MODEL), F32), pltpu.SemaphoreType.DMA((2,))],
        compiler_params=_cparams(("arbitrary",)),
        name="moe_combine",
    )(dest, dest, x1, wts, ys, ln_g.reshape(1, -1), ln_b.reshape(1, -1))


def _moe(x1, eid, wts, w1, w3, w2, layer, ln_g, ln_b):
    T = x1.shape[0]
    rt, counts = _route_ranks(eid)
    cnt = counts[0, :N_EXPERTS].astype(jnp.int32)
    padded = (cnt + _FFN_TILE - 1) // _FFN_TILE * _FFN_TILE
    ends = jnp.cumsum(padded)
    offsets = ends - padded
    n_rows = 2 * T + (N_EXPERTS + 1) * _FFN_TILE
    n_tiles = n_rows // _FFN_TILE
    tile_start = jnp.arange(n_tiles, dtype=jnp.int32) * _FFN_TILE
    tile_expert = jnp.minimum(jnp.sum(tile_start[:, None] >= ends[None, :], axis=1), N_EXPERTS - 1)
    n_used = (ends[-1:] // _FFN_TILE).astype(jnp.int32)
    dest = _route_dests(offsets.astype(jnp.int32), rt)
    xs = _scatter_rows((offsets + cnt).astype(jnp.int32), dest, x1, n_rows)
    ys = _expert_ffn(tile_expert.astype(jnp.int32), n_used, xs, w1, w3, w2, layer)
    return _combine(dest, x1, wts, ys, ln_g, ln_b)


def _router_weights(router_group, router_group_b, router_expert, router_expert_b):
    we = jnp.transpose(router_expert, (1, 0, 2)).reshape(D_MODEL, N_EXPERTS)
    w = jnp.concatenate([router_group, we], axis=1)
    b = jnp.concatenate([router_group_b, router_expert_b.reshape(N_EXPERTS)])
    pad = LANES - _ROUTE_COLS
    w = jnp.pad(w, ((0, 0), (0, pad)))
    w_hi = w.astype(BF16)
    w_lo = (w - w_hi.astype(F32)).astype(BF16)
    return jnp.concatenate([w_hi, w_lo], axis=1), jnp.pad(b, (0, pad)).reshape(1, LANES)


def kernel(x, positions, w_in, conv_w, conv_b, conv_ln_g, conv_ln_b, da_lambda, da_norm_g, hg_lb, hg_norm_g, w_branch, b_gate, w_out, ln1_g, ln1_b, router_group, router_group_b, router_expert, router_expert_b, exp_w1, exp_w3, exp_w2, ln2_g, ln2_b):
    B, S, D = x.shape
    T = B * S
    rope = _rope_tables(positions)
    lb_tab = jnp.cumsum(jax.nn.softmax(hg_lb.astype(F32), axis=0), axis=0)
    lb_tab = lb_tab - lb_tab[0:1]
    x2 = x.reshape(T, D)
    for l in range(DEPTH):
        a_in, q, k, v, hq, hf, hi, hg, gates = _in_projection(x2, w_in[l].astype(BF16), rope)
        y_a = _conformer_conv(a_in, conv_w[l], conv_b[l], conv_ln_g[l], conv_ln_b[l], B, S)
        y_b = _diff_attention(q, k, v, da_lambda[l], da_norm_g[l], l, B, S)
        y_c = _hgrn2(hq, hf, hi, hg, lb_tab[l], hg_norm_g[l], B, S)
        w_route, b_route = _router_weights(router_group[l], router_group_b[l],
                                           router_expert[l], router_expert_b[l])
        x1, eid, wts = _merge(x2, y_a, y_b, y_c, gates, w_branch[l].astype(BF16), b_gate[l],
                          w_out[l].astype(BF16), ln1_g[l], ln1_b[l], w_route, b_route)
        x2 = _moe(x1, eid, wts, exp_w1, exp_w3, exp_w2, l, ln2_g[l], ln2_b[l])
    return x2.reshape(B, S, D)
```

```python
import functools
import math

import jax
import jax.numpy as jnp
from jax import lax
from jax.experimental import pallas as pl
from jax.experimental.pallas import tpu as pltpu

F32 = jnp.float32
BF16 = jnp.bfloat16

D_MODEL = 1024
DEPTH = 2
CONV_CH = 512
CONV_K = 31
DA_HEADS = 4
DA_SUB = 64
ROT_DIM = DA_SUB // 4
ROPE_THETA = 500000.0
HG_HEADS = 4
HG_DK = 128
MIX_WIDTH = 512
N_BRANCH = 3
N_GROUPS = 4
EXPERTS_PER_GROUP = 8
N_EXPERTS = N_GROUPS * EXPERTS_PER_GROUP
D_EXPERT = 512
DN_ALPHA = (2 * DEPTH) ** 0.25
LN_EPS = 1e-5
IN_COLS = 2 * CONV_CH + 7 * MIX_WIDTH + N_BRANCH * D_MODEL

LANES = 128
SUBLANES = 8
VMEM_LIMIT_BYTES = 56 * 1024 * 1024

NEG_BIG = -1e30


def _cparams(sem):
    return pltpu.CompilerParams(dimension_semantics=sem, vmem_limit_bytes=VMEM_LIMIT_BYTES)


def _layer_norm_rows(r, g, b):
    mu = jnp.mean(r, axis=-1, keepdims=True)
    d = r - mu
    var = jnp.mean(d * d, axis=-1, keepdims=True)
    return d * lax.rsqrt(var + LN_EPS) * g + b


def _sigmoid(x):
    return 1.0 / (1.0 + jnp.exp(-x))


def _rope_kernel(pos_ref, freq_ref, c_ref, sa_ref, sb_ref):
    ang = pos_ref[...].astype(F32) * freq_ref[...]
    lane = lax.broadcasted_iota(jnp.int32, ang.shape, 1) & (DA_SUB - 1)
    half = ROT_DIM // 2
    cs = jnp.cos(ang)
    sn = jnp.sin(ang)
    c_ref[...] = jnp.where(lane < ROT_DIM, cs, 1.0)
    sa_ref[...] = jnp.where(lane < half, -sn, 0.0)
    sb_ref[...] = jnp.where((lane >= half) & (lane < ROT_DIM), sn, 0.0)


def _rope_tables(positions):
    T = positions.size
    tm = min(T, 1024)
    inv_freq = ROPE_THETA ** (-jnp.arange(0, ROT_DIM, 2, dtype=F32) / ROT_DIM)
    freq_lane = jnp.tile(inv_freq, LANES // (ROT_DIM // 2))[None, :]
    pos = positions.reshape(T, 1)
    out = jax.ShapeDtypeStruct((T, LANES), F32)
    return pl.pallas_call(
        _rope_kernel,
        grid=(T // tm,),
        in_specs=[pl.BlockSpec((tm, 1), lambda i: (i, 0)),
                  pl.BlockSpec((1, LANES), lambda i: (0, 0))],
        out_specs=[pl.BlockSpec((tm, LANES), lambda i: (i, 0))] * 3,
        out_shape=[out, out, out],
        compiler_params=_cparams(("parallel",)),
        name="rope_tables",
    )(pos, freq_lane)


_COL_A = 0
_COL_Q = 2 * CONV_CH
_COL_K = _COL_Q + MIX_WIDTH
_COL_V = _COL_K + MIX_WIDTH
_COL_HQ = _COL_V + MIX_WIDTH
_COL_HF = _COL_HQ + MIX_WIDTH
_COL_HI = _COL_HF + MIX_WIDTH
_COL_HG = _COL_HI + MIX_WIDTH
_COL_GATE = _COL_HG + MIX_WIDTH


def _inproj_kernel(x_ref, w_ref, c_ref, sa_ref, sb_ref,
                   a_ref, q_ref, k_ref, v_ref, hq_ref, hf_ref, hi_ref, hg_ref, gt_ref):
    xb = x_ref[...].astype(BF16)

    def mm(c0, width):
        return jnp.dot(xb, w_ref[:, c0:c0 + width], preferred_element_type=F32)

    def rotary(t, scale):
        c, sa, sb = c_ref[...], sa_ref[...], sb_ref[...]
        outs = []
        for h in range(DA_HEADS):
            th = t[:, h * LANES:(h + 1) * LANES]
            r = th * c + pltpu.roll(th, LANES - ROT_DIM // 2, 1) * sa + pltpu.roll(th, ROT_DIM // 2, 1) * sb
            outs.append(r * scale)
        return jnp.concatenate(outs, axis=1)

    for j in range(2):
        a_ref[:, j * CONV_CH:(j + 1) * CONV_CH] = mm(_COL_A + j * CONV_CH, CONV_CH).astype(BF16)
    q_ref[...] = rotary(mm(_COL_Q, MIX_WIDTH), DA_SUB ** -0.5 * math.log2(math.e)).astype(BF16)
    k_ref[...] = rotary(mm(_COL_K, MIX_WIDTH), 1.0).astype(BF16)
    v_ref[...] = mm(_COL_V, MIX_WIDTH).astype(BF16)
    hq_ref[...] = mm(_COL_HQ, MIX_WIDTH).astype(BF16)
    hf_ref[...] = mm(_COL_HF, MIX_WIDTH).astype(BF16)
    hi_ref[...] = mm(_COL_HI, MIX_WIDTH).astype(BF16)
    hg_ref[...] = mm(_COL_HG, MIX_WIDTH).astype(BF16)
    for j in range(N_BRANCH * D_MODEL // MIX_WIDTH):
        gt_ref[:, j * MIX_WIDTH:(j + 1) * MIX_WIDTH] = mm(_COL_GATE + j * MIX_WIDTH, MIX_WIDTH).astype(BF16)


def _in_projection(x2, w_in_bf16, rope):
    T = x2.shape[0]
    tm = min(T, 512)
    widths = [2 * CONV_CH] + [MIX_WIDTH] * 7 + [N_BRANCH * D_MODEL]
    row = lambda i: (i, 0)
    return pl.pallas_call(
        _inproj_kernel,
        grid=(T // tm,),
        in_specs=[pl.BlockSpec((tm, D_MODEL), row),
                  pl.BlockSpec((D_MODEL, IN_COLS), lambda i: (0, 0), pipeline_mode=pl.Buffered(1)),
                  pl.BlockSpec((tm, LANES), row),
                  pl.BlockSpec((tm, LANES), row),
                  pl.BlockSpec((tm, LANES), row)],
        out_specs=[pl.BlockSpec((tm, w), row) for w in widths],
        out_shape=[jax.ShapeDtypeStruct((T, w), BF16) for w in widths],
        compiler_params=_cparams(("parallel",)),
        name="in_projection",
    )(x2, w_in_bf16, *rope)


_HALO = 32
_CONV_ROWS = 64


def _conv_kernel(a_ref, halo_ref, w_ref, cb_ref, g_ref, b_ref, o_ref, ush_ref, y_ref, *, ts):
    j = pl.program_id(1)

    def glu(a):
        a = a.astype(F32)
        return a[:, :CONV_CH] * _sigmoid(a[:, CONV_CH:])

    u_cur = glu(a_ref[...])
    u_halo = jnp.where(j > 0, glu(halo_ref[...]), 0.0)
    u_ext = jnp.concatenate([u_halo, u_cur], axis=0)
    n_al = ts + _HALO - SUBLANES
    ush_ref[0] = u_ext
    for s in range(1, SUBLANES):
        ush_ref[s, 0:n_al, :] = ush_ref[0, s:s + n_al, :]

    off0 = _HALO - (CONV_K - 1)

    for cb in range(CONV_CH // LANES):
        cs = slice(cb * LANES, (cb + 1) * LANES)
        wk = [jnp.broadcast_to(w_ref[k:k + 1, cs], (SUBLANES, LANES)) for k in range(CONV_K)]

        def rows(r, carry, cs=cs, wk=wk):
            r0 = pl.multiple_of(r * _CONV_ROWS, _CONV_ROWS)
            for rg in range(_CONV_ROWS // SUBLANES):
                acc = None
                for k in range(CONV_K):
                    off = off0 + k
                    s, al = off % SUBLANES, (off // SUBLANES) * SUBLANES
                    term = wk[k] * ush_ref[s, pl.ds(r0 + al + rg * SUBLANES, SUBLANES), cs]
                    acc = term if acc is None else acc + term
                y_ref[pl.ds(r0 + rg * SUBLANES, SUBLANES), cs] = acc
            return carry

        lax.fori_loop(0, ts // _CONV_ROWS, rows, 0)
    y = y_ref[...] + cb_ref[...]
    yn = _layer_norm_rows(y, g_ref[...], b_ref[...])
    o_ref[...] = (yn * _sigmoid(yn)).astype(BF16)


def _conformer_conv(a_in, conv_w, conv_b, ln_g, ln_b, B, S):
    T = B * S
    ts = min(S, 512)
    nt = S // ts
    vec = lambda v: v.reshape(1, CONV_CH)
    return pl.pallas_call(
        functools.partial(_conv_kernel, ts=ts),
        grid=(B, nt),
        in_specs=[pl.BlockSpec((ts, 2 * CONV_CH), lambda b, j: (b * nt + j, 0)),
                  pl.BlockSpec((_HALO, 2 * CONV_CH),
                               lambda b, j: (jnp.maximum((b * nt + j) * (ts // _HALO) - 1, 0), 0)),
                  pl.BlockSpec((CONV_K, CONV_CH), lambda b, j: (0, 0)),
                  pl.BlockSpec((1, CONV_CH), lambda b, j: (0, 0)),
                  pl.BlockSpec((1, CONV_CH), lambda b, j: (0, 0)),
                  pl.BlockSpec((1, CONV_CH), lambda b, j: (0, 0))],
        out_specs=pl.BlockSpec((ts, CONV_CH), lambda b, j: (b * nt + j, 0)),
        out_shape=jax.ShapeDtypeStruct((T, CONV_CH), BF16),
        scratch_shapes=[pltpu.VMEM((SUBLANES, ts + _HALO, CONV_CH), F32),
                        pltpu.VMEM((ts, CONV_CH), F32)],
        compiler_params=_cparams(("parallel", "parallel")),
        name="conformer_conv",
    )(a_in, a_in, conv_w, vec(conv_b), vec(ln_g), vec(ln_b))


def _attn_kernel(q_ref, k_ref, v_ref, lam_ref, g_ref, o_ref, vt_ref, *, tq, tk, lam_init):
    qi = pl.program_id(2)
    n_diag = tq // tk

    @pl.when(qi == 0)
    def _():
        vt_ref[...] = v_ref[...].astype(F32).T.astype(BF16)

    q = q_ref[...]
    lane = lax.broadcasted_iota(jnp.int32, q.shape, 1)
    zero = jnp.zeros_like(q)
    qq = jnp.concatenate([jnp.where(lane < DA_SUB, q, zero), jnp.where(lane >= DA_SUB, q, zero)], axis=0)

    def scores(j):
        k0 = pl.multiple_of(j * tk, tk)
        return lax.dot_general(k_ref[pl.ds(k0, tk), :], qq, (((1,), (1,)), ((), ())),
                               preferred_element_type=F32)

    def update(j, st, carry):
        m, l, acc = carry
        k0 = pl.multiple_of(j * tk, tk)
        m_new = jnp.maximum(m, jnp.max(st, axis=0, keepdims=True))
        alpha = jnp.exp2(m - m_new)
        p = jnp.exp2(st - m_new)
        l = alpha * l + jnp.sum(p, axis=0, keepdims=True)
        acc = alpha * acc + jnp.dot(vt_ref[:, pl.ds(k0, tk)], p.astype(BF16), preferred_element_type=F32)
        return m_new, l, acc

    def body(j, c):
        st_next = scores(j + 1)
        m, l, acc = update(j, c[3], c[0:3])
        return m, l, acc, st_next

    n_full = qi * n_diag
    init = (jnp.full((1, 2 * tq), NEG_BIG, F32), jnp.zeros((1, 2 * tq), F32),
            jnp.zeros((LANES, 2 * tq), F32), scores(0))
    m, l, acc, st = lax.fori_loop(0, n_full, body, init)
    q_in = lax.broadcasted_iota(jnp.int32, (1, tq), 1)
    qpos = jnp.concatenate([q_in, q_in], axis=1)
    for d in range(n_diag):
        st_next = scores(n_full + d + 1) if d + 1 < n_diag else None
        kpos = d * tk + lax.broadcasted_iota(jnp.int32, (tk, 1), 0)
        m, l, acc = update(n_full + d, jnp.where(kpos <= qpos, st, NEG_BIG), (m, l, acc))
        st = st_next
    o12 = acc / l
    lp = lam_ref[...]
    lam = (jnp.exp(jnp.sum(lp[0:1] * lp[1:2], axis=1, keepdims=True))
           - jnp.exp(jnp.sum(lp[2:3] * lp[3:4], axis=1, keepdims=True)) + lam_init)
    o = o12[:, :tq] - lam * o12[:, tq:]
    ms = jnp.mean(o * o, axis=0, keepdims=True)
    o = o * lax.rsqrt(ms + LN_EPS) * g_ref[...] * (1.0 - lam_init)
    o_ref[...] = o.T.astype(BF16)


def _diff_attention(q, k, v, lam_params, norm_g, layer_idx, B, S):
    T = B * S
    tq = min(S, 512)
    tk = min(S, 256)
    nq = S // tq
    lam_init = 0.8 - 0.6 * math.exp(-0.3 * layer_idx)
    return pl.pallas_call(
        functools.partial(_attn_kernel, tq=tq, tk=tk, lam_init=lam_init),
        grid=(B, DA_HEADS, nq),
        in_specs=[pl.BlockSpec((tq, LANES), lambda b, h, i: (b * nq + i, h)),
                  pl.BlockSpec((S, LANES), lambda b, h, i: (b, h)),
                  pl.BlockSpec((S, LANES), lambda b, h, i: (b, h)),
                  pl.BlockSpec((4, DA_SUB), lambda b, h, i: (0, 0)),
                  pl.BlockSpec((LANES, 1), lambda b, h, i: (0, 0))],
        out_specs=pl.BlockSpec((tq, LANES), lambda b, h, i: (b * nq + i, h)),
        out_shape=jax.ShapeDtypeStruct((T, MIX_WIDTH), BF16),
        scratch_shapes=[pltpu.VMEM((LANES, S), BF16)],
        compiler_params=_cparams(("parallel", "parallel", "arbitrary")),
        name="diff_attention",
    )(q, k, v, lam_params, norm_g.reshape(LANES, 1))


_HG_CHUNK = 64


_HG_SAFE_DECAY = 160.0


def _hgrn_kernel(q_ref, f_ref, i_ref, g_ref, lb_ref, ng_ref, o_ref, st_ref, b_ref, kf_ref, *, n_chunks):
    C = _HG_CHUNK

    @pl.when(pl.program_id(1) == 0)
    def _():
        st_ref[...] = jnp.zeros_like(st_ref)

    rr = lax.broadcasted_iota(jnp.int32, (C, C), 0)
    cc = lax.broadcasted_iota(jnp.int32, (C, C), 1)
    causal = rr >= cc
    tri = jnp.where(causal, 1.0, 0.0).astype(BF16)
    tiles = [(slice(ci * C, (ci + 1) * C), h, slice(h * HG_DK, (h + 1) * HG_DK))
             for ci in range(n_chunks) for h in range(HG_HEADS)]

    decay = jnp.zeros((1, HG_DK), F32)
    for rows, h, cols in tiles:
        z = f_ref[rows, cols].astype(F32)
        lb = lb_ref[:, cols]
        e = jnp.exp(-jnp.abs(z))
        log_sig = jnp.minimum(z, 0.0) - jnp.log(1.0 + e)
        ya = jnp.log(lb)
        yb = jnp.log(1.0 - lb) + log_sig
        log_f = jnp.maximum(ya, yb) + jnp.log(1.0 + jnp.exp(-jnp.abs(ya - yb)))
        kf_ref[rows, cols] = (1.0 - lb) * jnp.where(z >= 0.0, e, 1.0) / (1.0 + e)
        hi = log_f.astype(BF16)
        r1 = log_f - hi.astype(F32)
        mid = r1.astype(BF16)
        lo = (r1 - mid.astype(F32)).astype(BF16)
        b3 = jnp.dot(tri, jnp.concatenate([hi, mid, lo], axis=1), preferred_element_type=F32)
        bcum = b3[:, 0:HG_DK] + b3[:, HG_DK:2 * HG_DK] + b3[:, 2 * HG_DK:3 * HG_DK]
        b_ref[rows, cols] = bcum
        decay = jnp.maximum(decay, -bcum[C - 1:C, :])
    unsafe = jnp.max(decay) > _HG_SAFE_DECAY

    def pairwise_att(qf, bcum, r0, cols):
        def body(g, att):
            s0 = pl.multiple_of(g * SUBLANES, SUBLANES)
            kblk = kf_ref[pl.ds(r0 + s0, SUBLANES), cols]
            bblk = b_ref[pl.ds(r0 + s0, SUBLANES), cols]
            for j in range(SUBLANES):
                w = qf * kblk[j:j + 1, :] * jnp.exp(jnp.minimum(bcum - bblk[j:j + 1, :], 0.0))
                att = att + jnp.where(cc == s0 + j, jnp.sum(w, axis=1, keepdims=True), 0.0)
            return att
        return lax.fori_loop(0, C // SUBLANES, body, jnp.zeros((C, C), F32))

    def phase2(exact):
        for rows, h, cols in tiles:
            bcum = b_ref[rows, cols]
            kf = kf_ref[rows, cols]
            b_last = bcum[C - 1:C, :]
            qv = q_ref[rows, cols].astype(F32)
            qf = qv * _sigmoid(qv)
            qe = qf * jnp.exp(bcum)
            kdec = (kf * jnp.exp(b_last - bcum)).astype(BF16)
            vv = i_ref[rows, cols]
            if exact:
                att = pairwise_att(qf, bcum, rows.start, cols)
            else:
                mref = 0.5 * b_last
                qd = (qe * jnp.exp(-mref)).astype(BF16)
                kd = (kf * jnp.exp(mref - bcum)).astype(BF16)
                att = lax.dot_general(qd, kd, (((1,), (1,)), ((), ())), preferred_element_type=F32)
            att = jnp.where(causal, att, 0.0).astype(BF16)
            st = st_ref[h]
            o = (lax.dot_general(qe.astype(BF16), st.astype(BF16), (((1,), (1,)), ((), ())),
                                 preferred_element_type=F32)
                 + jnp.dot(att, vv, preferred_element_type=F32))
            st_ref[h] = st * jnp.exp(b_last) + lax.dot_general(
                vv, kdec, (((0,), (0,)), ((), ())), preferred_element_type=F32)
            ms = jnp.mean(o * o, axis=1, keepdims=True)
            gv = g_ref[rows, cols].astype(F32)
            o_ref[rows, cols] = (o * lax.rsqrt(ms + LN_EPS) * ng_ref[...] * (gv * _sigmoid(gv))).astype(BF16)

    @pl.when(unsafe)
    def _():
        phase2(True)

    @pl.when(jnp.logical_not(unsafe))
    def _():
        phase2(False)


def _hgrn2(hq, hf, hi, hg, lb, norm_g, B, S):
    T = B * S
    ts = min(S, 256)
    nt = S // ts
    blk = pl.BlockSpec((ts, MIX_WIDTH), lambda b, j: (b * nt + j, 0))
    return pl.pallas_call(
        functools.partial(_hgrn_kernel, n_chunks=ts // _HG_CHUNK),
        grid=(B, nt),
        in_specs=[blk, blk, blk, blk,
                  pl.BlockSpec((1, MIX_WIDTH), lambda b, j: (0, 0)),
                  pl.BlockSpec((1, HG_DK), lambda b, j: (0, 0))],
        out_specs=blk,
        out_shape=jax.ShapeDtypeStruct((T, MIX_WIDTH), BF16),
        scratch_shapes=[pltpu.VMEM((HG_HEADS, HG_DK, HG_DK), F32),
                        pltpu.VMEM((ts, MIX_WIDTH), F32),
                        pltpu.VMEM((ts, MIX_WIDTH), F32)],
        compiler_params=_cparams(("parallel", "arbitrary")),
        name="hgrn2",
    )(hq, hf, hi, hg, lb.reshape(1, MIX_WIDTH), norm_g.reshape(1, HG_DK))


_ROUTE_COLS = N_GROUPS + N_EXPERTS


def _merge_kernel(x_ref, ya_ref, yb_ref, yc_ref, gt_ref, wbr_ref, bg_ref, wout_ref, lng_ref, lnb_ref,
                  wr_ref, br_ref, x1_ref, eid_ref, wts_ref):
    mix = None
    for n, y_ref in enumerate((ya_ref, yb_ref, yc_ref)):
        proj = jnp.dot(y_ref[...], wbr_ref[n], preferred_element_type=F32)
        gate = _sigmoid(gt_ref[:, n * D_MODEL:(n + 1) * D_MODEL].astype(F32) + bg_ref[n:n + 1, :])
        mix = gate * proj if mix is None else mix + gate * proj
    mo = jnp.dot(mix.astype(BF16), wout_ref[...], preferred_element_type=F32)
    x1 = _layer_norm_rows(DN_ALPHA * x_ref[...] + mo, lng_ref[...], lnb_ref[...])
    x1_ref[...] = x1

    xh = x1.astype(BF16)
    xl = (x1 - xh.astype(F32)).astype(BF16)
    pa = jnp.dot(xh, wr_ref[...], preferred_element_type=F32)
    pb = jnp.dot(xl, wr_ref[:, 0:LANES], preferred_element_type=F32)
    logits = pa[:, 0:LANES] + pa[:, LANES:2 * LANES] + pb + br_ref[...]
    lane = lax.broadcasted_iota(jnp.int32, logits.shape, 1)
    big = jnp.int32(1 << 20)
    is_g = lane < N_GROUPS
    gl = jnp.where(is_g, logits, NEG_BIG)
    gmax = jnp.max(gl, axis=1, keepdims=True)
    g_idx = jnp.min(jnp.where(is_g & (gl == gmax), lane, big), axis=1, keepdims=True)
    p_group = 1.0 / jnp.sum(jnp.where(is_g, jnp.exp(gl - gmax), 0.0), axis=1, keepdims=True)
    lo = N_GROUPS + EXPERTS_PER_GROUP * g_idx
    sel = (lane >= lo) & (lane < lo + EXPERTS_PER_GROUP)
    el = jnp.where(sel, logits, NEG_BIG)
    v1 = jnp.max(el, axis=1, keepdims=True)
    i1 = jnp.min(jnp.where(sel & (el == v1), lane, big), axis=1, keepdims=True)
    sel2 = sel & (lane != i1)
    el2 = jnp.where(sel2, logits, NEG_BIG)
    v2 = jnp.max(el2, axis=1, keepdims=True)
    i2 = jnp.min(jnp.where(sel2 & (el2 == v2), lane, big), axis=1, keepdims=True)
    e2 = jnp.exp(v2 - v1)
    w1 = p_group / (1.0 + e2)
    w2 = p_group * e2 / (1.0 + e2)
    eid_ref[...] = jnp.where(lane == 0, i1 - N_GROUPS, jnp.where(lane == 1, i2 - N_GROUPS, 0))
    wts_ref[...] = jnp.where(lane == 0, w1, jnp.where(lane == 1, w2, 0.0))


def _merge(x2, ya, yb, yc, gates, w_branch, b_gate, w_out, ln_g, ln_b, w_route, b_route):
    T = x2.shape[0]
    tm = min(T, 512)
    row = lambda i: (i, 0)
    const2 = lambda i: (0, 0)
    return pl.pallas_call(
        _merge_kernel,
        grid=(T // tm,),
        in_specs=[pl.BlockSpec((tm, D_MODEL), row),
                  pl.BlockSpec((tm, MIX_WIDTH), row),
                  pl.BlockSpec((tm, MIX_WIDTH), row),
                  pl.BlockSpec((tm, MIX_WIDTH), row),
                  pl.BlockSpec((tm, N_BRANCH * D_MODEL), row),
                  pl.BlockSpec((N_BRANCH, MIX_WIDTH, D_MODEL), lambda i: (0, 0, 0)),
                  pl.BlockSpec((N_BRANCH, D_MODEL), const2),
                  pl.BlockSpec((D_MODEL, D_MODEL), const2),
                  pl.BlockSpec((1, D_MODEL), const2),
                  pl.BlockSpec((1, D_MODEL), const2),
                  pl.BlockSpec((D_MODEL, 2 * LANES), const2),
                  pl.BlockSpec((1, LANES), const2)],
        out_specs=[pl.BlockSpec((tm, D_MODEL), row), pl.BlockSpec((tm, LANES), row),
                   pl.BlockSpec((tm, LANES), row)],
        out_shape=[jax.ShapeDtypeStruct((T, D_MODEL), F32), jax.ShapeDtypeStruct((T, LANES), jnp.int32),
                   jax.ShapeDtypeStruct((T, LANES), F32)],
        compiler_params=_cparams(("parallel",)),
        name="merge_router",
    )(x2, ya, yb, yc, gates, w_branch, b_gate, w_out, ln_g.reshape(1, -1), ln_b.reshape(1, -1),
      w_route, b_route)


_TOK_TILE = 256
_FFN_TILE = 512
_RANK_TILES_PER_STEP = 4


def _rank_kernel(eid_ref, rt_ref, cnt_ref, base_ref):
    n_sub, _, tm = rt_ref.shape

    @pl.when(pl.program_id(0) == 0)
    def _():
        base_ref[...] = jnp.zeros_like(base_ref)

    lane = lax.broadcasted_iota(jnp.int32, (tm, LANES), 1)
    rr = lax.broadcasted_iota(jnp.int32, (tm, tm), 0)
    cc = lax.broadcasted_iota(jnp.int32, (tm, tm), 1)
    before = jnp.where(rr > cc, 1.0, 0.0).astype(BF16)
    base = base_ref[...]
    for u in range(n_sub):
        e1 = eid_ref[u * tm:(u + 1) * tm, 0:1]
        e2 = eid_ref[u * tm:(u + 1) * tm, 1:2]
        oh1 = lane == e1
        oh2 = lane == e2
        oh = jnp.where(oh1 | oh2, 1.0, 0.0)
        pref = jnp.dot(before, oh.astype(BF16), preferred_element_type=F32) + base
        r1 = jnp.sum(jnp.where(oh1, pref, 0.0), axis=1, keepdims=True)
        r2 = jnp.sum(jnp.where(oh2, pref, 0.0), axis=1, keepdims=True)
        slab = jnp.where(lane == 0, e1.astype(F32),
                         jnp.where(lane == 1, e2.astype(F32),
                                   jnp.where(lane == 2, r1, jnp.where(lane == 3, r2, 0.0))))
        rt_ref[u] = slab.T[0:SUBLANES, :].astype(jnp.int32)
        base = base + jnp.sum(oh, axis=0, keepdims=True)
    base_ref[...] = base
    cnt_ref[...] = base


def _route_ranks(eid):
    T = eid.shape[0]
    tm = min(T, _TOK_TILE)
    nt = T // tm
    n_sub = math.gcd(nt, _RANK_TILES_PER_STEP)
    return pl.pallas_call(
        _rank_kernel,
        grid=(nt // n_sub,),
        in_specs=[pl.BlockSpec((n_sub * tm, LANES), lambda i: (i, 0))],
        out_specs=[pl.BlockSpec((n_sub, SUBLANES, tm), lambda i: (i, 0, 0)),
                   pl.BlockSpec((1, LANES), lambda i: (0, 0))],
        out_shape=[jax.ShapeDtypeStruct((nt, SUBLANES, tm), jnp.int32),
                   jax.ShapeDtypeStruct((1, LANES), F32)],
        scratch_shapes=[pltpu.VMEM((1, LANES), F32)],
        compiler_params=_cparams(("arbitrary",)),
        name="route_ranks",
    )(eid)


def _dest_kernel(off_ref, rt_ref, d_ref):
    e = rt_ref[:, 0:2, :]
    dest = rt_ref[:, 2:4, :]
    for k in range(N_EXPERTS):
        dest = dest + jnp.where(e == k, off_ref[k], 0)
    d_ref[...] = dest


def _route_dests(offsets, rt):
    nt, _, tm = rt.shape
    return pl.pallas_call(
        _dest_kernel,
        grid_spec=pltpu.PrefetchScalarGridSpec(
            num_scalar_prefetch=1, grid=(1,),
            in_specs=[pl.BlockSpec((nt, SUBLANES, tm), lambda i, off: (0, 0, 0))],
            out_specs=pl.BlockSpec((nt, 2, tm), lambda i, off: (0, 0, 0))),
        out_shape=jax.ShapeDtypeStruct((nt, 2, tm), jnp.int32),
        compiler_params=_cparams(("arbitrary",)),
        name="route_dests",
    )(offsets, rt)


def _row_copy(src, dst, sem):
    return pltpu.make_async_copy(src, dst, sem)


def _scatter_kernel(tails_ref, dest_ref, x1_ref, xs_ref, xbuf_ref, zero_ref, sems):
    i = pl.program_id(0)
    nt = pl.num_programs(0)
    tm = x1_ref.shape[0]
    slot = i % 2

    def zero_copy(e):
        start = pl.multiple_of(tails_ref[e] // SUBLANES * SUBLANES, SUBLANES)
        return _row_copy(zero_ref, xs_ref.at[pl.ds(start, _FFN_TILE)], sems.at[2])

    @pl.when(i == 0)
    def _():
        zero_ref[...] = jnp.zeros_like(zero_ref)
        for e in range(N_EXPERTS):
            zero_copy(e).start()
        for e in range(N_EXPERTS):
            zero_copy(e).wait()

    xbuf_ref[slot] = x1_ref[...]

    for t in range(tm):
        for k in range(2):
            _row_copy(xbuf_ref.at[slot, pl.ds(t, 1)], xs_ref.at[pl.ds(dest_ref[0, k, t], 1)],
                      sems.at[slot]).start()

    def drain(s):
        for _ in range(2):
            _row_copy(xbuf_ref.at[s], xs_ref.at[pl.ds(0, tm)], sems.at[s]).wait()

    @pl.when(i > 0)
    def _():
        drain(1 - slot)

    @pl.when(i == nt - 1)
    def _():
        drain(slot)


def _scatter_rows(tails, dest, x1, n_rows):
    nt, _, tm = dest.shape
    return pl.pallas_call(
        _scatter_kernel,
        grid_spec=pltpu.PrefetchScalarGridSpec(
            num_scalar_prefetch=1, grid=(nt,),
            in_specs=[pl.BlockSpec((1, 2, tm), lambda i, tl: (i, 0, 0), memory_space=pltpu.SMEM),
                      pl.BlockSpec((tm, D_MODEL), lambda i, tl: (i, 0))],
            out_specs=pl.BlockSpec(memory_space=pl.ANY),
            scratch_shapes=[pltpu.VMEM((2, tm, D_MODEL), F32),
                            pltpu.VMEM((_FFN_TILE, D_MODEL), F32),
                            pltpu.SemaphoreType.DMA((3,))]),
        out_shape=jax.ShapeDtypeStruct((n_rows, D_MODEL), F32),
        compiler_params=_cparams(("arbitrary",)),
        name="moe_scatter",
    )(tails, dest, x1)


def _ffn_kernel(te_ref, nu_ref, xs_ref, w1_ref, w3_ref, w2_ref, ys_ref, w13_ref, w2b_ref):
    i = pl.program_id(0)

    @pl.when((i == 0) | (te_ref[i] != te_ref[jnp.maximum(i - 1, 0)]))
    def _():
        w13_ref[0] = w1_ref[0, 0].astype(BF16)
        w13_ref[1] = w3_ref[0, 0].astype(BF16)
        w2b_ref[...] = w2_ref[0, 0].astype(BF16)

    @pl.when(i < nu_ref[0])
    def _():
        xb = xs_ref[...].astype(BF16)
        h1 = jnp.dot(xb, w13_ref[0], preferred_element_type=F32)
        h3 = jnp.dot(xb, w13_ref[1], preferred_element_type=F32)
        hh = (h1 * _sigmoid(h1) * h3).astype(BF16)
        ys_ref[...] = jnp.dot(hh, w2b_ref[...], preferred_element_type=F32)


def _expert_ffn(tile_expert, n_used, xs, w1, w3, w2, layer):
    n_rows = xs.shape[0]
    n_tiles = n_rows // _FFN_TILE
    rows = lambda i, te, nu: (jnp.minimum(i, nu[0] - 1), 0)
    wsel = lambda i, te, nu: (layer, te[jnp.minimum(i, n_tiles - 1)], 0, 0)
    return pl.pallas_call(
        _ffn_kernel,
        grid_spec=pltpu.PrefetchScalarGridSpec(
            num_scalar_prefetch=2, grid=(n_tiles,),
            in_specs=[pl.BlockSpec((_FFN_TILE, D_MODEL), rows),
                      pl.BlockSpec((1, 1, D_MODEL, D_EXPERT), wsel),
                      pl.BlockSpec((1, 1, D_MODEL, D_EXPERT), wsel),
                      pl.BlockSpec((1, 1, D_EXPERT, D_MODEL), wsel)],
            out_specs=pl.BlockSpec((_FFN_TILE, D_MODEL), rows),
            scratch_shapes=[pltpu.VMEM((2, D_MODEL, D_EXPERT), BF16),
                            pltpu.VMEM((D_EXPERT, D_MODEL), BF16)]),
        out_shape=jax.ShapeDtypeStruct((n_rows, D_MODEL), F32),
        compiler_params=_cparams(("arbitrary",)),
        name="moe_expert_ffn",
    )(tile_expert, n_used, xs, w1, w3, w2)


def _combine_kernel(dcur_ref, dnext_ref, x1_ref, wts_ref, ys_ref, lng_ref, lnb_ref, o_ref, ybuf_ref, sems):
    i = pl.program_id(0)
    nt = pl.num_programs(0)
    tm = x1_ref.shape[0]
    slot = i % 2

    def issue(d_ref, s):
        for t in range(tm):
            for k in range(2):
                _row_copy(ys_ref.at[pl.ds(d_ref[0, k, t], 1)], ybuf_ref.at[s, pl.ds(k * tm + t, 1)],
                          sems.at[s]).start()

    @pl.when(i == 0)
    def _():
        issue(dcur_ref, 0)

    @pl.when(i + 1 < nt)
    def _():
        issue(dnext_ref, 1 - slot)

    _row_copy(ys_ref.at[pl.ds(0, 2 * tm)], ybuf_ref.at[slot], sems.at[slot]).wait()

    w1 = wts_ref[:, 0:1]
    w2 = wts_ref[:, 1:2]
    moe = w1 * ybuf_ref[slot, 0:tm, :] + w2 * ybuf_ref[slot, tm:2 * tm, :]
    o_ref[...] = _layer_norm_rows(DN_ALPHA * x1_ref[...] + moe, lng_ref[...], lnb_ref[...])


def _combine(dest, x1, wts, ys, ln_g, ln_b):
    T = x1.shape[0]
    nt, _, tm = dest.shape
    dspec = lambda f: pl.BlockSpec((1, 2, tm), f, memory_space=pltpu.SMEM)
    return pl.pallas_call(
        _combine_kernel,
        grid=(nt,),
        in_specs=[dspec(lambda i: (i, 0, 0)),
                  dspec(lambda i: (jnp.minimum(i + 1, nt - 1), 0, 0)),
                  pl.BlockSpec((tm, D_MODEL), lambda i: (i, 0)),
                  pl.BlockSpec((tm, LANES), lambda i: (i, 0)),
                  pl.BlockSpec(memory_space=pl.ANY),
                  pl.BlockSpec((1, D_MODEL), lambda i: (0, 0)),
                  pl.BlockSpec((1, D_MODEL), lambda i: (0, 0))],
        out_specs=pl.BlockSpec((tm, D_MODEL), lambda i: (i, 0)),
        out_shape=jax.ShapeDtypeStruct((T, D_MODEL), F32),
        scratch_shapes=[pltpu.VMEM((2, 2 * tm, D_MODEL), F32), pltpu.SemaphoreType.DMA((2,))],
        compiler_params=_cparams(("arbitrary",)),
        name="moe_combine",
    )(dest, dest, x1, wts, ys, ln_g.reshape(1, -1), ln_b.reshape(1, -1))


def _moe(x1, eid, wts, w1, w3, w2, layer, ln_g, ln_b):
    T = x1.shape[0]
    rt, counts = _route_ranks(eid)
    cnt = counts[0, :N_EXPERTS].astype(jnp.int32)
    padded = (cnt + _FFN_TILE - 1) // _FFN_TILE * _FFN_TILE
    ends = jnp.cumsum(padded)
    offsets = ends - padded
    n_rows = 2 * T + (N_EXPERTS + 1) * _FFN_TILE
    n_tiles = n_rows // _FFN_TILE
    tile_start = jnp.arange(n_tiles, dtype=jnp.int32) * _FFN_TILE
    tile_expert = jnp.minimum(jnp.sum(tile_start[:, None] >= ends[None, :], axis=1), N_EXPERTS - 1)
    n_used = (ends[-1:] // _FFN_TILE).astype(jnp.int32)
    dest = _route_dests(offsets.astype(jnp.int32), rt)
    xs = _scatter_rows((offsets + cnt).astype(jnp.int32), dest, x1, n_rows)
    ys = _expert_ffn(tile_expert.astype(jnp.int32), n_used, xs, w1, w3, w2, layer)
    return _combine(dest, x1, wts, ys, ln_g, ln_b)


def _router_weights(router_group, router_group_b, router_expert, router_expert_b):
    we = jnp.transpose(router_expert, (1, 0, 2)).reshape(D_MODEL, N_EXPERTS)
    w = jnp.concatenate([router_group, we], axis=1)
    b = jnp.concatenate([router_group_b, router_expert_b.reshape(N_EXPERTS)])
    pad = LANES - _ROUTE_COLS
    w = jnp.pad(w, ((0, 0), (0, pad)))
    w_hi = w.astype(BF16)
    w_lo = (w - w_hi.astype(F32)).astype(BF16)
    return jnp.concatenate([w_hi, w_lo], axis=1), jnp.pad(b, (0, pad)).reshape(1, LANES)


def kernel(x, positions, w_in, conv_w, conv_b, conv_ln_g, conv_ln_b, da_lambda, da_norm_g, hg_lb, hg_norm_g, w_branch, b_gate, w_out, ln1_g, ln1_b, router_group, router_group_b, router_expert, router_expert_b, exp_w1, exp_w3, exp_w2, ln2_g, ln2_b):
    B, S, D = x.shape
    T = B * S
    rope = _rope_tables(positions)
    lb_tab = jnp.cumsum(jax.nn.softmax(hg_lb.astype(F32), axis=0), axis=0)
    lb_tab = lb_tab - lb_tab[0:1]
    x2 = x.reshape(T, D)
    for l in range(DEPTH):
        a_in, q, k, v, hq, hf, hi, hg, gates = _in_projection(x2, w_in[l].astype(BF16), rope)
        y_a = _conformer_conv(a_in, conv_w[l], conv_b[l], conv_ln_g[l], conv_ln_b[l], B, S)
        y_b = _diff_attention(q, k, v, da_lambda[l], da_norm_g[l], l, B, S)
        y_c = _hgrn2(hq, hf, hi, hg, lb_tab[l], hg_norm_g[l], B, S)
        w_route, b_route = _router_weights(router_group[l], router_group_b[l],
                                           router_expert[l], router_expert_b[l])
        x1, eid, wts = _merge(x2, y_a, y_b, y_c, gates, w_branch[l].astype(BF16), b_gate[l],
                          w_out[l].astype(BF16), ln1_g[l], ln1_b[l], w_route, b_route)
        x2 = _moe(x1, eid, wts, exp_w1, exp_w3, exp_w2, l, ln2_g[l], ln2_b[l])
    return x2.reshape(B, S, D)
```

```python
import functools
import math

import jax
import jax.numpy as jnp
from jax import lax
from jax.experimental import pallas as pl
from jax.experimental.pallas import tpu as pltpu

F32 = jnp.float32
BF16 = jnp.bfloat16

D_MODEL = 1024
DEPTH = 2
CONV_CH = 512
CONV_K = 31
DA_HEADS = 4
DA_SUB = 64
ROT_DIM = DA_SUB // 4
ROPE_THETA = 500000.0
HG_HEADS = 4
HG_DK = 128
MIX_WIDTH = 512
N_BRANCH = 3
N_GROUPS = 4
EXPERTS_PER_GROUP = 8
N_EXPERTS = N_GROUPS * EXPERTS_PER_GROUP
D_EXPERT = 512
DN_ALPHA = (2 * DEPTH) ** 0.25
LN_EPS = 1e-5
IN_COLS = 2 * CONV_CH + 7 * MIX_WIDTH + N_BRANCH * D_MODEL

LANES = 128
SUBLANES = 8
VMEM_LIMIT_BYTES = 56 * 1024 * 1024

NEG_BIG = -1e30


def _cparams(sem):
    return pltpu.CompilerParams(dimension_semantics=sem, vmem_limit_bytes=VMEM_LIMIT_BYTES)


def _layer_norm_rows(r, g, b):
    mu = jnp.mean(r, axis=-1, keepdims=True)
    d = r - mu
    var = jnp.mean(d * d, axis=-1, keepdims=True)
    return d * lax.rsqrt(var + LN_EPS) * g + b


def _sigmoid(x):
    return 1.0 / (1.0 + jnp.exp(-x))


def _rope_kernel(pos_ref, freq_ref, c_ref, sa_ref, sb_ref):
    ang = pos_ref[...].astype(F32) * freq_ref[...]
    lane = lax.broadcasted_iota(jnp.int32, ang.shape, 1) & (DA_SUB - 1)
    half = ROT_DIM // 2
    cs = jnp.cos(ang)
    sn = jnp.sin(ang)
    c_ref[...] = jnp.where(lane < ROT_DIM, cs, 1.0)
    sa_ref[...] = jnp.where(lane < half, -sn, 0.0)
    sb_ref[...] = jnp.where((lane >= half) & (lane < ROT_DIM), sn, 0.0)


def _rope_tables(positions):
    T = positions.size
    tm = min(T, 1024)
    inv_freq = ROPE_THETA ** (-jnp.arange(0, ROT_DIM, 2, dtype=F32) / ROT_DIM)
    freq_lane = jnp.tile(inv_freq, LANES // (ROT_DIM // 2))[None, :]
    pos = positions.reshape(T, 1)
    out = jax.ShapeDtypeStruct((T, LANES), F32)
    return pl.pallas_call(
        _rope_kernel,
        grid=(T // tm,),
        in_specs=[pl.BlockSpec((tm, 1), lambda i: (i, 0)),
                  pl.BlockSpec((1, LANES), lambda i: (0, 0))],
        out_specs=[pl.BlockSpec((tm, LANES), lambda i: (i, 0))] * 3,
        out_shape=[out, out, out],
        compiler_params=_cparams(("parallel",)),
        name="rope_tables",
    )(pos, freq_lane)


_COL_A = 0
_COL_Q = 2 * CONV_CH
_COL_K = _COL_Q + MIX_WIDTH
_COL_V = _COL_K + MIX_WIDTH
_COL_HQ = _COL_V + MIX_WIDTH
_COL_HF = _COL_HQ + MIX_WIDTH
_COL_HI = _COL_HF + MIX_WIDTH
_COL_HG = _COL_HI + MIX_WIDTH
_COL_GATE = _COL_HG + MIX_WIDTH


def _inproj_kernel(x_ref, w_ref, c_ref, sa_ref, sb_ref,
                   a_ref, q_ref, k_ref, v_ref, hq_ref, hf_ref, hi_ref, hg_ref, gt_ref):
    xb = x_ref[...].astype(BF16)

    def mm(c0, width):
        return jnp.dot(xb, w_ref[:, c0:c0 + width], preferred_element_type=F32)

    def rotary(t, scale):
        c, sa, sb = c_ref[...], sa_ref[...], sb_ref[...]
        outs = []
        for h in range(DA_HEADS):
            th = t[:, h * LANES:(h + 1) * LANES]
            r = th * c + pltpu.roll(th, LANES - ROT_DIM // 2, 1) * sa + pltpu.roll(th, ROT_DIM // 2, 1) * sb
            outs.append(r * scale)
        return jnp.concatenate(outs, axis=1)

    for j in range(2):
        a_ref[:, j * CONV_CH:(j + 1) * CONV_CH] = mm(_COL_A + j * CONV_CH, CONV_CH).astype(BF16)
    q_ref[...] = rotary(mm(_COL_Q, MIX_WIDTH), DA_SUB ** -0.5 * math.log2(math.e)).astype(BF16)
    k_ref[...] = rotary(mm(_COL_K, MIX_WIDTH), 1.0).astype(BF16)
    v_ref[...] = mm(_COL_V, MIX_WIDTH).astype(BF16)
    hq_ref[...] = mm(_COL_HQ, MIX_WIDTH).astype(BF16)
    hf_ref[...] = mm(_COL_HF, MIX_WIDTH).astype(BF16)
    hi_ref[...] = mm(_COL_HI, MIX_WIDTH).astype(BF16)
    hg_ref[...] = mm(_COL_HG, MIX_WIDTH).astype(BF16)
    for j in range(N_BRANCH * D_MODEL // MIX_WIDTH):
        gt_ref[:, j * MIX_WIDTH:(j + 1) * MIX_WIDTH] = mm(_COL_GATE + j * MIX_WIDTH, MIX_WIDTH).astype(BF16)


def _in_projection(x2, w_in_bf16, rope):
    T = x2.shape[0]
    tm = min(T, 512)
    widths = [2 * CONV_CH] + [MIX_WIDTH] * 7 + [N_BRANCH * D_MODEL]
    row = lambda i: (i, 0)
    return pl.pallas_call(
        _inproj_kernel,
        grid=(T // tm,),
        in_specs=[pl.BlockSpec((tm, D_MODEL), row),
                  pl.BlockSpec((D_MODEL, IN_COLS), lambda i: (0, 0), pipeline_mode=pl.Buffered(1)),
                  pl.BlockSpec((tm, LANES), row),
                  pl.BlockSpec((tm, LANES), row),
                  pl.BlockSpec((tm, LANES), row)],
        out_specs=[pl.BlockSpec((tm, w), row) for w in widths],
        out_shape=[jax.ShapeDtypeStruct((T, w), BF16) for w in widths],
        compiler_params=_cparams(("parallel",)),
        name="in_projection",
    )(x2, w_in_bf16, *rope)


_HALO = 32
_CONV_ROWS = 64


def _conv_kernel(a_ref, halo_ref, w_ref, cb_ref, g_ref, b_ref, o_ref, ush_ref, y_ref, *, ts):
    j = pl.program_id(1)

    def glu(a):
        a = a.astype(F32)
        return a[:, :CONV_CH] * _sigmoid(a[:, CONV_CH:])

    u_cur = glu(a_ref[...])
    u_halo = jnp.where(j > 0, glu(halo_ref[...]), 0.0)
    u_ext = jnp.concatenate([u_halo, u_cur], axis=0)
    n_al = ts + _HALO - SUBLANES
    ush_ref[0] = u_ext
    for s in range(1, SUBLANES):
        ush_ref[s, 0:n_al, :] = ush_ref[0, s:s + n_al, :]

    off0 = _HALO - (CONV_K - 1)

    for cb in range(CONV_CH // LANES):
        cs = slice(cb * LANES, (cb + 1) * LANES)
        wk = [jnp.broadcast_to(w_ref[k:k + 1, cs], (SUBLANES, LANES)) for k in range(CONV_K)]

        def rows(r, carry, cs=cs, wk=wk):
            r0 = pl.multiple_of(r * _CONV_ROWS, _CONV_ROWS)
            for rg in range(_CONV_ROWS // SUBLANES):
                acc = None
                for k in range(CONV_K):
                    off = off0 + k
                    s, al = off % SUBLANES, (off // SUBLANES) * SUBLANES
                    term = wk[k] * ush_ref[s, pl.ds(r0 + al + rg * SUBLANES, SUBLANES), cs]
                    acc = term if acc is None else acc + term
                y_ref[pl.ds(r0 + rg * SUBLANES, SUBLANES), cs] = acc
            return carry

        lax.fori_loop(0, ts // _CONV_ROWS, rows, 0)
    y = y_ref[...] + cb_ref[...]
    yn = _layer_norm_rows(y, g_ref[...], b_ref[...])
    o_ref[...] = (yn * _sigmoid(yn)).astype(BF16)


def _conformer_conv(a_in, conv_w, conv_b, ln_g, ln_b, B, S):
    T = B * S
    ts = min(S, 512)
    nt = S // ts
    vec = lambda v: v.reshape(1, CONV_CH)
    return pl.pallas_call(
        functools.partial(_conv_kernel, ts=ts),
        grid=(B, nt),
        in_specs=[pl.BlockSpec((ts, 2 * CONV_CH), lambda b, j: (b * nt + j, 0)),
                  pl.BlockSpec((_HALO, 2 * CONV_CH),
                               lambda b, j: (jnp.maximum((b * nt + j) * (ts // _HALO) - 1, 0), 0)),
                  pl.BlockSpec((CONV_K, CONV_CH), lambda b, j: (0, 0)),
                  pl.BlockSpec((1, CONV_CH), lambda b, j: (0, 0)),
                  pl.BlockSpec((1, CONV_CH), lambda b, j: (0, 0)),
                  pl.BlockSpec((1, CONV_CH), lambda b, j: (0, 0))],
        out_specs=pl.BlockSpec((ts, CONV_CH), lambda b, j: (b * nt + j, 0)),
        out_shape=jax.ShapeDtypeStruct((T, CONV_CH), BF16),
        scratch_shapes=[pltpu.VMEM((SUBLANES, ts + _HALO, CONV_CH), F32),
                        pltpu.VMEM((ts, CONV_CH), F32)],
        compiler_params=_cparams(("parallel", "parallel")),
        name="conformer_conv",
    )(a_in, a_in, conv_w, vec(conv_b), vec(ln_g), vec(ln_b))


def _attn_kernel(q_ref, k_ref, v_ref, lam_ref, g_ref, o_ref, vt_ref, *, tq, tk, lam_init):
    qi = pl.program_id(2)
    n_diag = tq // tk

    @pl.when(qi == 0)
    def _():
        vt_ref[...] = v_ref[...].astype(F32).T.astype(BF16)

    q = q_ref[...]
    lane = lax.broadcasted_iota(jnp.int32, q.shape, 1)
    zero = jnp.zeros_like(q)
    qq = jnp.concatenate([jnp.where(lane < DA_SUB, q, zero), jnp.where(lane >= DA_SUB, q, zero)], axis=0)

    def scores(j):
        k0 = pl.multiple_of(j * tk, tk)
        return lax.dot_general(k_ref[pl.ds(k0, tk), :], qq, (((1,), (1,)), ((), ())),
                               preferred_element_type=F32)

    def update(j, st, carry):
        m, l, acc = carry
        k0 = pl.multiple_of(j * tk, tk)
        m_new = jnp.maximum(m, jnp.max(st, axis=0, keepdims=True))
        alpha = jnp.exp2(m - m_new)
        p = jnp.exp2(st - m_new)
        l = alpha * l + jnp.sum(p, axis=0, keepdims=True)
        acc = alpha * acc + jnp.dot(vt_ref[:, pl.ds(k0, tk)], p.astype(BF16), preferred_element_type=F32)
        return m_new, l, acc

    def body(j, c):
        st_next = scores(j + 1)
        m, l, acc = update(j, c[3], c[0:3])
        return m, l, acc, st_next

    n_full = qi * n_diag
    init = (jnp.full((1, 2 * tq), NEG_BIG, F32), jnp.zeros((1, 2 * tq), F32),
            jnp.zeros((LANES, 2 * tq), F32), scores(0))
    m, l, acc, st = lax.fori_loop(0, n_full, body, init)
    q_in = lax.broadcasted_iota(jnp.int32, (1, tq), 1)
    qpos = jnp.concatenate([q_in, q_in], axis=1)
    for d in range(n_diag):
        st_next = scores(n_full + d + 1) if d + 1 < n_diag else None
        kpos = d * tk + lax.broadcasted_iota(jnp.int32, (tk, 1), 0)
        m, l, acc = update(n_full + d, jnp.where(kpos <= qpos, st, NEG_BIG), (m, l, acc))
        st = st_next
    o12 = acc / l
    lp = lam_ref[...]
    lam = (jnp.exp(jnp.sum(lp[0:1] * lp[1:2], axis=1, keepdims=True))
           - jnp.exp(jnp.sum(lp[2:3] * lp[3:4], axis=1, keepdims=True)) + lam_init)
    o = o12[:, :tq] - lam * o12[:, tq:]
    ms = jnp.mean(o * o, axis=0, keepdims=True)
    o = o * lax.rsqrt(ms + LN_EPS) * g_ref[...] * (1.0 - lam_init)
    o_ref[...] = o.T.astype(BF16)


def _diff_attention(q, k, v, lam_params, norm_g, layer_idx, B, S):
    T = B * S
    tq = min(S, 512)
    tk = min(S, 256)
    nq = S // tq
    lam_init = 0.8 - 0.6 * math.exp(-0.3 * layer_idx)
    return pl.pallas_call(
        functools.partial(_attn_kernel, tq=tq, tk=tk, lam_init=lam_init),
        grid=(B, DA_HEADS, nq),
        in_specs=[pl.BlockSpec((tq, LANES), lambda b, h, i: (b * nq + i, h)),
                  pl.BlockSpec((S, LANES), lambda b, h, i: (b, h)),
                  pl.BlockSpec((S, LANES), lambda b, h, i: (b, h)),
                  pl.BlockSpec((4, DA_SUB), lambda b, h, i: (0, 0)),
                  pl.BlockSpec((LANES, 1), lambda b, h, i: (0, 0))],
        out_specs=pl.BlockSpec((tq, LANES), lambda b, h, i: (b * nq + i, h)),
        out_shape=jax.ShapeDtypeStruct((T, MIX_WIDTH), BF16),
        scratch_shapes=[pltpu.VMEM((LANES, S), BF16)],
        compiler_params=_cparams(("parallel", "parallel", "arbitrary")),
        name="diff_attention",
    )(q, k, v, lam_params, norm_g.reshape(LANES, 1))


_HG_CHUNK = 64


_HG_SAFE_DECAY = 160.0


def _hgrn_kernel(q_ref, f_ref, i_ref, g_ref, lb_ref, ng_ref, o_ref, st_ref, b_ref, kf_ref, *, n_chunks):
    C = _HG_CHUNK

    @pl.when(pl.program_id(1) == 0)
    def _():
        st_ref[...] = jnp.zeros_like(st_ref)

    rr = lax.broadcasted_iota(jnp.int32, (C, C), 0)
    cc = lax.broadcasted_iota(jnp.int32, (C, C), 1)
    causal = rr >= cc
    tri = jnp.where(causal, 1.0, 0.0).astype(BF16)
    tiles = [(slice(ci * C, (ci + 1) * C), h, slice(h * HG_DK, (h + 1) * HG_DK))
             for ci in range(n_chunks) for h in range(HG_HEADS)]

    decay = jnp.zeros((1, HG_DK), F32)
    for rows, h, cols in tiles:
        z = f_ref[rows, cols].astype(F32)
        lb = lb_ref[:, cols]
        e = jnp.exp(-jnp.abs(z))
        log_sig = jnp.minimum(z, 0.0) - jnp.log(1.0 + e)
        ya = jnp.log(lb)
        yb = jnp.log(1.0 - lb) + log_sig
        log_f = jnp.maximum(ya, yb) + jnp.log(1.0 + jnp.exp(-jnp.abs(ya - yb)))
        kf_ref[rows, cols] = (1.0 - lb) * jnp.where(z >= 0.0, e, 1.0) / (1.0 + e)
        hi = log_f.astype(BF16)
        r1 = log_f - hi.astype(F32)
        mid = r1.astype(BF16)
        lo = (r1 - mid.astype(F32)).astype(BF16)
        b3 = jnp.dot(tri, jnp.concatenate([hi, mid, lo], axis=1), preferred_element_type=F32)
        bcum = b3[:, 0:HG_DK] + b3[:, HG_DK:2 * HG_DK] + b3[:, 2 * HG_DK:3 * HG_DK]
        b_ref[rows, cols] = bcum
        decay = jnp.maximum(decay, -bcum[C - 1:C, :])
    unsafe = jnp.max(decay) > _HG_SAFE_DECAY

    def pairwise_att(qf, bcum, r0, cols):
        def body(g, att):
            s0 = pl.multiple_of(g * SUBLANES, SUBLANES)
            kblk = kf_ref[pl.ds(r0 + s0, SUBLANES), cols]
            bblk = b_ref[pl.ds(r0 + s0, SUBLANES), cols]
            for j in range(SUBLANES):
                w = qf * kblk[j:j + 1, :] * jnp.exp(jnp.minimum(bcum - bblk[j:j + 1, :], 0.0))
                att = att + jnp.where(cc == s0 + j, jnp.sum(w, axis=1, keepdims=True), 0.0)
            return att
        return lax.fori_loop(0, C // SUBLANES, body, jnp.zeros((C, C), F32))

    def phase2(exact):
        for rows, h, cols in tiles:
            bcum = b_ref[rows, cols]
            kf = kf_ref[rows, cols]
            b_last = bcum[C - 1:C, :]
            qv = q_ref[rows, cols].astype(F32)
            qf = qv * _sigmoid(qv)
            qe = qf * jnp.exp(bcum)
            kdec = (kf * jnp.exp(b_last - bcum)).astype(BF16)
            vv = i_ref[rows, cols]
            if exact:
                att = pairwise_att(qf, bcum, rows.start, cols)
            else:
                mref = 0.5 * b_last
                qd = (qe * jnp.exp(-mref)).astype(BF16)
                kd = (kf * jnp.exp(mref - bcum)).astype(BF16)
                att = lax.dot_general(qd, kd, (((1,), (1,)), ((), ())), preferred_element_type=F32)
            att = jnp.where(causal, att, 0.0).astype(BF16)
            st = st_ref[h]
            o = (lax.dot_general(qe.astype(BF16), st.astype(BF16), (((1,), (1,)), ((), ())),
                                 preferred_element_type=F32)
                 + jnp.dot(att, vv, preferred_element_type=F32))
            st_ref[h] = st * jnp.exp(b_last) + lax.dot_general(
                vv, kdec, (((0,), (0,)), ((), ())), preferred_element_type=F32)
            ms = jnp.mean(o * o, axis=1, keepdims=True)
            gv = g_ref[rows, cols].astype(F32)
            o_ref[rows, cols] = (o * lax.rsqrt(ms + LN_EPS) * ng_ref[...] * (gv * _sigmoid(gv))).astype(BF16)

    @pl.when(unsafe)
    def _():
        phase2(True)

    @pl.when(jnp.logical_not(unsafe))
    def _():
        phase2(False)


def _hgrn2(hq, hf, hi, hg, lb, norm_g, B, S):
    T = B * S
    ts = min(S, 256)
    nt = S // ts
    blk = pl.BlockSpec((ts, MIX_WIDTH), lambda b, j: (b * nt + j, 0))
    return pl.pallas_call(
        functools.partial(_hgrn_kernel, n_chunks=ts // _HG_CHUNK),
        grid=(B, nt),
        in_specs=[blk, blk, blk, blk,
                  pl.BlockSpec((1, MIX_WIDTH), lambda b, j: (0, 0)),
                  pl.BlockSpec((1, HG_DK), lambda b, j: (0, 0))],
        out_specs=blk,
        out_shape=jax.ShapeDtypeStruct((T, MIX_WIDTH), BF16),
        scratch_shapes=[pltpu.VMEM((HG_HEADS, HG_DK, HG_DK), F32),
                        pltpu.VMEM((ts, MIX_WIDTH), F32),
                        pltpu.VMEM((ts, MIX_WIDTH), F32)],
        compiler_params=_cparams(("parallel", "arbitrary")),
        name="hgrn2",
    )(hq, hf, hi, hg, lb.reshape(1, MIX_WIDTH), norm_g.reshape(1, HG_DK))


_ROUTE_COLS = N_GROUPS + N_EXPERTS


def _merge_kernel(x_ref, ya_ref, yb_ref, yc_ref, gt_ref, wbr_ref, bg_ref, wout_ref, lng_ref, lnb_ref,
                  wr_ref, br_ref, x1_ref, eid_ref, wts_ref):
    mix = None
    for n, y_ref in enumerate((ya_ref, yb_ref, yc_ref)):
        proj = jnp.dot(y_ref[...], wbr_ref[n], preferred_element_type=F32)
        gate = _sigmoid(gt_ref[:, n * D_MODEL:(n + 1) * D_MODEL].astype(F32) + bg_ref[n:n + 1, :])
        mix = gate * proj if mix is None else mix + gate * proj
    mo = jnp.dot(mix.astype(BF16), wout_ref[...], preferred_element_type=F32)
    x1 = _layer_norm_rows(DN_ALPHA * x_ref[...] + mo, lng_ref[...], lnb_ref[...])
    x1_ref[...] = x1

    xh = x1.astype(BF16)
    xl = (x1 - xh.astype(F32)).astype(BF16)
    pa = jnp.dot(xh, wr_ref[...], preferred_element_type=F32)
    pb = jnp.dot(xl, wr_ref[:, 0:LANES], preferred_element_type=F32)
    logits = pa[:, 0:LANES] + pa[:, LANES:2 * LANES] + pb + br_ref[...]
    lane = lax.broadcasted_iota(jnp.int32, logits.shape, 1)
    big = jnp.int32(1 << 20)
    is_g = lane < N_GROUPS
    gl = jnp.where(is_g, logits, NEG_BIG)
    gmax = jnp.max(gl, axis=1, keepdims=True)
    g_idx = jnp.min(jnp.where(is_g & (gl == gmax), lane, big), axis=1, keepdims=True)
    p_group = 1.0 / jnp.sum(jnp.where(is_g, jnp.exp(gl - gmax), 0.0), axis=1, keepdims=True)
    lo = N_GROUPS + EXPERTS_PER_GROUP * g_idx
    sel = (lane >= lo) & (lane < lo + EXPERTS_PER_GROUP)
    el = jnp.where(sel, logits, NEG_BIG)
    v1 = jnp.max(el, axis=1, keepdims=True)
    i1 = jnp.min(jnp.where(sel & (el == v1), lane, big), axis=1, keepdims=True)
    sel2 = sel & (lane != i1)
    el2 = jnp.where(sel2, logits, NEG_BIG)
    v2 = jnp.max(el2, axis=1, keepdims=True)
    i2 = jnp.min(jnp.where(sel2 & (el2 == v2), lane, big), axis=1, keepdims=True)
    e2 = jnp.exp(v2 - v1)
    w1 = p_group / (1.0 + e2)
    w2 = p_group * e2 / (1.0 + e2)
    eid_ref[...] = jnp.where(lane == 0, i1 - N_GROUPS, jnp.where(lane == 1, i2 - N_GROUPS, 0))
    wts_ref[...] = jnp.where(lane == 0, w1, jnp.where(lane == 1, w2, 0.0))


def _merge(x2, ya, yb, yc, gates, w_branch, b_gate, w_out, ln_g, ln_b, w_route, b_route):
    T = x2.shape[0]
    tm = min(T, 512)
    row = lambda i: (i, 0)
    const2 = lambda i: (0, 0)
    return pl.pallas_call(
        _merge_kernel,
        grid=(T // tm,),
        in_specs=[pl.BlockSpec((tm, D_MODEL), row),
                  pl.BlockSpec((tm, MIX_WIDTH), row),
                  pl.BlockSpec((tm, MIX_WIDTH), row),
                  pl.BlockSpec((tm, MIX_WIDTH), row),
                  pl.BlockSpec((tm, N_BRANCH * D_MODEL), row),
                  pl.BlockSpec((N_BRANCH, MIX_WIDTH, D_MODEL), lambda i: (0, 0, 0)),
                  pl.BlockSpec((N_BRANCH, D_MODEL), const2),
                  pl.BlockSpec((D_MODEL, D_MODEL), const2),
                  pl.BlockSpec((1, D_MODEL), const2),
                  pl.BlockSpec((1, D_MODEL), const2),
                  pl.BlockSpec((D_MODEL, 2 * LANES), const2),
                  pl.BlockSpec((1, LANES), const2)],
        out_specs=[pl.BlockSpec((tm, D_MODEL), row), pl.BlockSpec((tm, LANES), row),
                   pl.BlockSpec((tm, LANES), row)],
        out_shape=[jax.ShapeDtypeStruct((T, D_MODEL), F32), jax.ShapeDtypeStruct((T, LANES), jnp.int32),
                   jax.ShapeDtypeStruct((T, LANES), F32)],
        compiler_params=_cparams(("parallel",)),
        name="merge_router",
    )(x2, ya, yb, yc, gates, w_branch, b_gate, w_out, ln_g.reshape(1, -1), ln_b.reshape(1, -1),
      w_route, b_route)


_TOK_TILE = 256
_FFN_TILE = 512
_RANK_TILES_PER_STEP = 4


def _rank_kernel(eid_ref, rt_ref, cnt_ref, base_ref):
    n_sub, _, tm = rt_ref.shape

    @pl.when(pl.program_id(0) == 0)
    def _():
        base_ref[...] = jnp.zeros_like(base_ref)

    lane = lax.broadcasted_iota(jnp.int32, (tm, LANES), 1)
    rr = lax.broadcasted_iota(jnp.int32, (tm, tm), 0)
    cc = lax.broadcasted_iota(jnp.int32, (tm, tm), 1)
    before = jnp.where(rr > cc, 1.0, 0.0).astype(BF16)
    base = base_ref[...]
    for u in range(n_sub):
        e1 = eid_ref[u * tm:(u + 1) * tm, 0:1]
        e2 = eid_ref[u * tm:(u + 1) * tm, 1:2]
        oh1 = lane == e1
        oh2 = lane == e2
        oh = jnp.where(oh1 | oh2, 1.0, 0.0)
        pref = jnp.dot(before, oh.astype(BF16), preferred_element_type=F32) + base
        r1 = jnp.sum(jnp.where(oh1, pref, 0.0), axis=1, keepdims=True)
        r2 = jnp.sum(jnp.where(oh2, pref, 0.0), axis=1, keepdims=True)
        slab = jnp.where(lane == 0, e1.astype(F32),
                         jnp.where(lane == 1, e2.astype(F32),
                                   jnp.where(lane == 2, r1, jnp.where(lane == 3, r2, 0.0))))
        rt_ref[u] = slab.T[0:SUBLANES, :].astype(jnp.int32)
        base = base + jnp.sum(oh, axis=0, keepdims=True)
    base_ref[...] = base
    cnt_ref[...] = base


def _route_ranks(eid):
    T = eid.shape[0]
    tm = min(T, _TOK_TILE)
    nt = T // tm
    n_sub = math.gcd(nt, _RANK_TILES_PER_STEP)
    return pl.pallas_call(
        _rank_kernel,
        grid=(nt // n_sub,),
        in_specs=[pl.BlockSpec((n_sub * tm, LANES), lambda i: (i, 0))],
        out_specs=[pl.BlockSpec((n_sub, SUBLANES, tm), lambda i: (i, 0, 0)),
                   pl.BlockSpec((1, LANES), lambda i: (0, 0))],
        out_shape=[jax.ShapeDtypeStruct((nt, SUBLANES, tm), jnp.int32),
                   jax.ShapeDtypeStruct((1, LANES), F32)],
        scratch_shapes=[pltpu.VMEM((1, LANES), F32)],
        compiler_params=_cparams(("arbitrary",)),
        name="route_ranks",
    )(eid)


def _dest_kernel(off_ref, rt_ref, d_ref):
    e = rt_ref[:, 0:2, :]
    dest = rt_ref[:, 2:4, :]
    for k in range(N_EXPERTS):
        dest = dest + jnp.where(e == k, off_ref[k], 0)
    d_ref[...] = dest


def _route_dests(offsets, rt):
    nt, _, tm = rt.shape
    return pl.pallas_call(
        _dest_kernel,
        grid_spec=pltpu.PrefetchScalarGridSpec(
            num_scalar_prefetch=1, grid=(1,),
            in_specs=[pl.BlockSpec((nt, SUBLANES, tm), lambda i, off: (0, 0, 0))],
            out_specs=pl.BlockSpec((nt, 2, tm), lambda i, off: (0, 0, 0))),
        out_shape=jax.ShapeDtypeStruct((nt, 2, tm), jnp.int32),
        compiler_params=_cparams(("arbitrary",)),
        name="route_dests",
    )(offsets, rt)


def _row_copy(src, dst, sem):
    return pltpu.make_async_copy(src, dst, sem)


def _scatter_kernel(tails_ref, dest_ref, x1_ref, xs_ref, xbuf_ref, zero_ref, sems):
    i = pl.program_id(0)
    nt = pl.num_programs(0)
    tm = x1_ref.shape[0]
    slot = i % 2

    def zero_copy(e):
        start = pl.multiple_of(tails_ref[e] // SUBLANES * SUBLANES, SUBLANES)
        return _row_copy(zero_ref, xs_ref.at[pl.ds(start, _FFN_TILE)], sems.at[2])

    @pl.when(i == 0)
    def _():
        zero_ref[...] = jnp.zeros_like(zero_ref)
        for e in range(N_EXPERTS):
            zero_copy(e).start()
        for e in range(N_EXPERTS):
            zero_copy(e).wait()

    xbuf_ref[slot] = x1_ref[...]

    for t in range(tm):
        for k in range(2):
            _row_copy(xbuf_ref.at[slot, pl.ds(t, 1)], xs_ref.at[pl.ds(dest_ref[0, k, t], 1)],
                      sems.at[slot]).start(priority=k)

    def drain(s):
        for _ in range(2):
            _row_copy(xbuf_ref.at[s], xs_ref.at[pl.ds(0, tm)], sems.at[s]).wait()

    @pl.when(i > 0)
    def _():
        drain(1 - slot)

    @pl.when(i == nt - 1)
    def _():
        drain(slot)


def _scatter_rows(tails, dest, x1, n_rows):
    nt, _, tm = dest.shape
    return pl.pallas_call(
        _scatter_kernel,
        grid_spec=pltpu.PrefetchScalarGridSpec(
            num_scalar_prefetch=1, grid=(nt,),
            in_specs=[pl.BlockSpec((1, 2, tm), lambda i, tl: (i, 0, 0), memory_space=pltpu.SMEM),
                      pl.BlockSpec((tm, D_MODEL), lambda i, tl: (i, 0))],
            out_specs=pl.BlockSpec(memory_space=pl.ANY),
            scratch_shapes=[pltpu.VMEM((2, tm, D_MODEL), F32),
                            pltpu.VMEM((_FFN_TILE, D_MODEL), F32),
                            pltpu.SemaphoreType.DMA((3,))]),
        out_shape=jax.ShapeDtypeStruct((n_rows, D_MODEL), F32),
        compiler_params=_cparams(("arbitrary",)),
        name="moe_scatter",
    )(tails, dest, x1)


def _ffn_kernel(te_ref, nu_ref, xs_ref, w1_ref, w3_ref, w2_ref, ys_ref, w13_ref, w2b_ref):
    i = pl.program_id(0)

    @pl.when((i == 0) | (te_ref[i] != te_ref[jnp.maximum(i - 1, 0)]))
    def _():
        w13_ref[0] = w1_ref[0, 0].astype(BF16)
        w13_ref[1] = w3_ref[0, 0].astype(BF16)
        w2b_ref[...] = w2_ref[0, 0].astype(BF16)

    @pl.when(i < nu_ref[0])
    def _():
        xb = xs_ref[...].astype(BF16)
        h1 = jnp.dot(xb, w13_ref[0], preferred_element_type=F32)
        h3 = jnp.dot(xb, w13_ref[1], preferred_element_type=F32)
        hh = (h1 * _sigmoid(h1) * h3).astype(BF16)
        ys_ref[...] = jnp.dot(hh, w2b_ref[...], preferred_element_type=F32)


def _expert_ffn(tile_expert, n_used, xs, w1, w3, w2, layer):
    n_rows = xs.shape[0]
    n_tiles = n_rows // _FFN_TILE
    rows = lambda i, te, nu: (jnp.minimum(i, nu[0] - 1), 0)
    wsel = lambda i, te, nu: (layer, te[jnp.minimum(i, n_tiles - 1)], 0, 0)
    return pl.pallas_call(
        _ffn_kernel,
        grid_spec=pltpu.PrefetchScalarGridSpec(
            num_scalar_prefetch=2, grid=(n_tiles,),
            in_specs=[pl.BlockSpec((_FFN_TILE, D_MODEL), rows),
                      pl.BlockSpec((1, 1, D_MODEL, D_EXPERT), wsel),
                      pl.BlockSpec((1, 1, D_MODEL, D_EXPERT), wsel),
                      pl.BlockSpec((1, 1, D_EXPERT, D_MODEL), wsel)],
            out_specs=pl.BlockSpec((_FFN_TILE, D_MODEL), rows),
            scratch_shapes=[pltpu.VMEM((2, D_MODEL, D_EXPERT), BF16),
                            pltpu.VMEM((D_EXPERT, D_MODEL), BF16)]),
        out_shape=jax.ShapeDtypeStruct((n_rows, D_MODEL), F32),
        compiler_params=_cparams(("arbitrary",)),
        name="moe_expert_ffn",
    )(tile_expert, n_used, xs, w1, w3, w2)


def _combine_kernel(dcur_ref, dnext_ref, x1_ref, wts_ref, ys_ref, lng_ref, lnb_ref, o_ref, ybuf_ref, sems):
    i = pl.program_id(0)
    nt = pl.num_programs(0)
    tm = x1_ref.shape[0]
    slot = i % 2

    def issue(d_ref, s):
        for t in range(tm):
            for k in range(2):
                _row_copy(ys_ref.at[pl.ds(d_ref[0, k, t], 1)], ybuf_ref.at[s, pl.ds(k * tm + t, 1)],
                          sems.at[s]).start(priority=k)

    @pl.when(i == 0)
    def _():
        issue(dcur_ref, 0)

    @pl.when(i + 1 < nt)
    def _():
        issue(dnext_ref, 1 - slot)

    _row_copy(ys_ref.at[pl.ds(0, 2 * tm)], ybuf_ref.at[slot], sems.at[slot]).wait()

    w1 = wts_ref[:, 0:1]
    w2 = wts_ref[:, 1:2]
    moe = w1 * ybuf_ref[slot, 0:tm, :] + w2 * ybuf_ref[slot, tm:2 * tm, :]
    o_ref[...] = _layer_norm_rows(DN_ALPHA * x1_ref[...] + moe, lng_ref[...], lnb_ref[...])


def _combine(dest, x1, wts, ys, ln_g, ln_b):
    T = x1.shape[0]
    nt, _, tm = dest.shape
    dspec = lambda f: pl.BlockSpec((1, 2, tm), f, memory_space=pltpu.SMEM)
    return pl.pallas_call(
        _combine_kernel,
        grid=(nt,),
        in_specs=[dspec(lambda i: (i, 0, 0)),
                  dspec(lambda i: (jnp.minimum(i + 1, nt - 1), 0, 0)),
                  pl.BlockSpec((tm, D_MODEL), lambda i: (i, 0)),
                  pl.BlockSpec((tm, LANES), lambda i: (i, 0)),
                  pl.BlockSpec(memory_space=pl.ANY),
                  pl.BlockSpec((1, D_MODEL), lambda i: (0, 0)),
                  pl.BlockSpec((1, D_MODEL), lambda i: (0, 0))],
        out_specs=pl.BlockSpec((tm, D_MODEL), lambda i: (i, 0)),
        out_shape=jax.ShapeDtypeStruct((T, D_MODEL), F32),
        scratch_shapes=[pltpu.VMEM((2, 2 * tm, D_MODEL), F32), pltpu.SemaphoreType.DMA((2,))],
        compiler_params=_cparams(("arbitrary",)),
        name="moe_combine",
    )(dest, dest, x1, wts, ys, ln_g.reshape(1, -1), ln_b.reshape(1, -1))


def _moe(x1, eid, wts, w1, w3, w2, layer, ln_g, ln_b):
    T = x1.shape[0]
    rt, counts = _route_ranks(eid)
    cnt = counts[0, :N_EXPERTS].astype(jnp.int32)
    padded = (cnt + _FFN_TILE - 1) // _FFN_TILE * _FFN_TILE
    ends = jnp.cumsum(padded)
    offsets = ends - padded
    n_rows = 2 * T + (N_EXPERTS + 1) * _FFN_TILE
    n_tiles = n_rows // _FFN_TILE
    tile_start = jnp.arange(n_tiles, dtype=jnp.int32) * _FFN_TILE
    tile_expert = jnp.minimum(jnp.sum(tile_start[:, None] >= ends[None, :], axis=1), N_EXPERTS - 1)
    n_used = (ends[-1:] // _FFN_TILE).astype(jnp.int32)
    dest = _route_dests(offsets.astype(jnp.int32), rt)
    xs = _scatter_rows((offsets + cnt).astype(jnp.int32), dest, x1, n_rows)
    ys = _expert_ffn(tile_expert.astype(jnp.int32), n_used, xs, w1, w3, w2, layer)
    return _combine(dest, x1, wts, ys, ln_g, ln_b)


def _router_weights(router_group, router_group_b, router_expert, router_expert_b):
    we = jnp.transpose(router_expert, (1, 0, 2)).reshape(D_MODEL, N_EXPERTS)
    w = jnp.concatenate([router_group, we], axis=1)
    b = jnp.concatenate([router_group_b, router_expert_b.reshape(N_EXPERTS)])
    pad = LANES - _ROUTE_COLS
    w = jnp.pad(w, ((0, 0), (0, pad)))
    w_hi = w.astype(BF16)
    w_lo = (w - w_hi.astype(F32)).astype(BF16)
    return jnp.concatenate([w_hi, w_lo], axis=1), jnp.pad(b, (0, pad)).reshape(1, LANES)


def kernel(x, positions, w_in, conv_w, conv_b, conv_ln_g, conv_ln_b, da_lambda, da_norm_g, hg_lb, hg_norm_g, w_branch, b_gate, w_out, ln1_g, ln1_b, router_group, router_group_b, router_expert, router_expert_b, exp_w1, exp_w3, exp_w2, ln2_g, ln2_b):
    B, S, D = x.shape
    T = B * S
    rope = _rope_tables(positions)
    lb_tab = jnp.cumsum(jax.nn.softmax(hg_lb.astype(F32), axis=0), axis=0)
    lb_tab = lb_tab - lb_tab[0:1]
    x2 = x.reshape(T, D)
    for l in range(DEPTH):
        a_in, q, k, v, hq, hf, hi, hg, gates = _in_projection(x2, w_in[l].astype(BF16), rope)
        y_a = _conformer_conv(a_in, conv_w[l], conv_b[l], conv_ln_g[l], conv_ln_b[l], B, S)
        y_b = _diff_attention(q, k, v, da_lambda[l], da_norm_g[l], l, B, S)
        y_c = _hgrn2(hq, hf, hi, hg, lb_tab[l], hg_norm_g[l], B, S)
        w_route, b_route = _router_weights(router_group[l], router_group_b[l],
                                           router_expert[l], router_expert_b[l])
        x1, eid, wts = _merge(x2, y_a, y_b, y_c, gates, w_branch[l].astype(BF16), b_gate[l],
                          w_out[l].astype(BF16), ln1_g[l], ln1_b[l], w_route, b_route)
        x2 = _moe(x1, eid, wts, exp_w1, exp_w3, exp_w2, l, ln2_g[l], ln2_b[l])
    return x2.reshape(B, S, D)
```
